```python
import math
import jax, jax.numpy as jnp
from jax import lax
import numpy as np

D_MODEL = 1024
BATCH = 8
SEQ = 4096
DEPTH = 1

MLA_HEADS = 8
MLA_NOPE_DIM = 64
MLA_ROPE_DIM = 32
MLA_V_DIM = 64
MLA_Q_RANK = 256
MLA_KV_RANK = 128
FOX_HEADS = 8
FOX_HEAD_DIM = 64
FOX_WIDTH = FOX_HEADS * FOX_HEAD_DIM
MIX_WIDTH = MLA_HEADS * MLA_V_DIM + FOX_WIDTH
IN_SPLITS = [MLA_Q_RANK, MLA_KV_RANK, MLA_ROPE_DIM, FOX_WIDTH, FOX_WIDTH, FOX_WIDTH, FOX_HEADS]
IN_WIDTH = sum(IN_SPLITS)
ROPE_THETA = 10000.0
Q_BLOCK = 128
N_EXPERTS = 32
TOP_K = 4
D_FF_EXPERT = D_MODEL
SWIGLU_ALPHA = 1.702
SWIGLU_LIMIT = 7.0
EXPERT_BLOCK = 512
RMS_EPS = 1e-6
NEG_INF = -1e30

kernel_name = 'hybrid_mla_fox_moe_adaln_layer'


def rms_norm(x, g):
    xf = x.astype(jnp.float32)
    y = xf * lax.rsqrt(jnp.mean(xf * xf, axis=-1, keepdims=True) + RMS_EPS)
    return (y * g.astype(jnp.float32)).astype(x.dtype)


def rope_tables(positions):
    inv_freq = ROPE_THETA ** (-jnp.arange(0, MLA_ROPE_DIM, 2, dtype=jnp.float32) / MLA_ROPE_DIM)
    ang = positions.astype(jnp.float32)[..., None] * inv_freq
    return jnp.cos(ang), jnp.sin(ang)


def apply_rope(x, cos, sin):
    half = x.shape[-1] // 2
    x1, x2 = x[..., :half], x[..., half:]
    cos = cos.astype(x.dtype)
    sin = sin.astype(x.dtype)
    return jnp.concatenate([x1 * cos - x2 * sin, x1 * sin + x2 * cos], axis=-1)


def causal_block_attention(q, k, v, scale, log_forget_cum=None):
    S = q.shape[2]
    outs = []
    for i in range(S // Q_BLOCK):
        q0, q1 = i * Q_BLOCK, (i + 1) * Q_BLOCK
        s = jnp.einsum('bhqd,bhkd->bhqk', q[:, :, q0:q1], k[:, :, :q1]).astype(jnp.float32) * scale
        if log_forget_cum is not None:
            s = s + (log_forget_cum[:, :, q0:q1, None] - log_forget_cum[:, :, None, :q1])
        q_pos = q0 + jnp.arange(Q_BLOCK)
        k_pos = jnp.arange(q1)
        s = jnp.where(k_pos[None, :] <= q_pos[:, None], s, NEG_INF)
        p = jax.nn.softmax(s, axis=-1).astype(v.dtype)
        outs.append(jnp.einsum('bhqk,bhkd->bhqd', p, v[:, :, :q1]))
    return jnp.concatenate(outs, axis=2)


def mixer_sublayer(x, shift, scale, gate, cos, sin, g_pre, g_post, w_in, g_q_norm, w_uq,
                   g_kv_norm, w_ukv, b_forget, w_out):
    B, S, _ = x.shape
    h = rms_norm(x, g_pre) * (1 + scale[:, None, :]) + shift[:, None, :]
    proj = h @ w_in
    idx = [int(v) for v in np.cumsum(IN_SPLITS)[:-1]]
    cq, ckv, k_rope, fq, fk, fv, f_logit = jnp.split(proj, idx, axis=-1)

    q = (rms_norm(cq, g_q_norm) @ w_uq).reshape(B, S, MLA_HEADS, MLA_NOPE_DIM + MLA_ROPE_DIM)
    q_nope, q_rope = q[..., :MLA_NOPE_DIM], q[..., MLA_NOPE_DIM:]
    q_rope = apply_rope(q_rope, cos[:, :, None, :], sin[:, :, None, :])
    kv = (rms_norm(ckv, g_kv_norm) @ w_ukv).reshape(B, S, MLA_HEADS, MLA_NOPE_DIM + MLA_V_DIM)
    k_nope, v_m = kv[..., :MLA_NOPE_DIM], kv[..., MLA_NOPE_DIM:]
    k_rope = apply_rope(k_rope, cos, sin)
    k_m = jnp.concatenate(
        [k_nope, jnp.broadcast_to(k_rope[:, :, None, :], (B, S, MLA_HEADS, MLA_ROPE_DIM))], axis=-1)
    q_m = jnp.concatenate([q_nope, q_rope], axis=-1)
    o_mla = causal_block_attention(q_m.transpose(0, 2, 1, 3), k_m.transpose(0, 2, 1, 3),
                                   v_m.transpose(0, 2, 1, 3),
                                   1.0 / math.sqrt(MLA_NOPE_DIM + MLA_ROPE_DIM))

    def heads(t):
        return t.reshape(B, S, FOX_HEADS, FOX_HEAD_DIM).transpose(0, 2, 1, 3)
    log_f = jax.nn.log_sigmoid(f_logit.astype(jnp.float32) + b_forget.astype(jnp.float32))
    f_cum = jnp.cumsum(log_f, axis=1).transpose(0, 2, 1)
    o_fox = causal_block_attention(heads(fq), heads(fk), heads(fv),
                                   1.0 / math.sqrt(FOX_HEAD_DIM), f_cum)

    o = jnp.concatenate([o_mla.transpose(0, 2, 1, 3).reshape(B, S, MLA_HEADS * MLA_V_DIM),
                         o_fox.transpose(0, 2, 1, 3).reshape(B, S, FOX_WIDTH)], axis=-1)
    o = o @ w_out
    return x + gate[:, None, :] * rms_norm(o, g_post)


def moe_sublayer(x, shift, scale, gate, g_pre, g_post, w_router, b_router,
                 w_gate_up, b_gate_up, w_down, b_down):
    B, S, D = x.shape
    T = B * S
    h = rms_norm(x, g_pre) * (1 + scale[:, None, :]) + shift[:, None, :]
    xf = h.reshape(T, D)
    logits = (xf @ w_router + b_router).astype(jnp.float32)
    top_vals, top_idx = lax.top_k(logits, TOP_K)
    gates = jax.nn.softmax(top_vals, axis=-1)

    A = T * TOP_K
    e_flat = top_idx.reshape(A).astype(jnp.int32)
    tok_flat = (jnp.arange(A, dtype=jnp.int32) // TOP_K)
    g_flat = gates.reshape(A)
    order = jnp.argsort(e_flat)
    e_sorted = e_flat[order]
    counts = jnp.bincount(e_flat, length=N_EXPERTS).astype(jnp.int32)
    starts = jnp.cumsum(counts) - counts
    pcounts = ((counts + EXPERT_BLOCK - 1) // EXPERT_BLOCK) * EXPERT_BLOCK
    pends = jnp.cumsum(pcounts)
    pstarts = pends - pcounts
    dest = pstarts[e_sorted] + (jnp.arange(A, dtype=jnp.int32) - starts[e_sorted])
    n_blocks = -(-A // EXPERT_BLOCK) + N_EXPERTS
    P = n_blocks * EXPERT_BLOCK
    buf_tok = jnp.full((P,), T, dtype=jnp.int32).at[dest].set(tok_flat[order])
    buf_gate = jnp.zeros((P,), dtype=x.dtype).at[dest].set(g_flat[order].astype(x.dtype))
    block_expert = jnp.minimum(
        jnp.searchsorted(pends, jnp.arange(n_blocks, dtype=jnp.int32) * EXPERT_BLOCK, side='right'),
        N_EXPERTS - 1).astype(jnp.int32)
    x_pad = jnp.concatenate([xf, jnp.zeros((1, D), dtype=xf.dtype)], axis=0)

    def expert_block(args):
        tok_b, e = args
        xb = x_pad[tok_b]
        gu = xb @ w_gate_up[e] + b_gate_up[e]
        g, u = gu[:, :D_FF_EXPERT], gu[:, D_FF_EXPERT:]
        g = jnp.minimum(g, SWIGLU_LIMIT)
        u = jnp.clip(u, -SWIGLU_LIMIT, SWIGLU_LIMIT)
        glu = g * jax.nn.sigmoid(SWIGLU_ALPHA * g)
        return ((u + 1) * glu) @ w_down[e] + b_down[e]

    y = lax.map(expert_block, (buf_tok.reshape(n_blocks, EXPERT_BLOCK), block_expert))
    y = y.reshape(P, D) * buf_gate[:, None]
    out = jnp.zeros((T + 1, D), dtype=y.dtype).at[buf_tok].add(y)[:T].reshape(B, S, D)
    return x + gate[:, None, :] * rms_norm(out, g_post)


def setup_inputs(seed: int = 0) -> dict:
    key = jax.random.key(seed)
    ks = jax.random.split(key, 32)
    L, D, E, F = DEPTH, D_MODEL, N_EXPERTS, D_FF_EXPERT
    nrm = jax.random.normal

    def gain(k, n):
        return 1.0 + 0.05 * nrm(k, (L, n), jnp.float32)

    x = nrm(ks[0], (BATCH, SEQ, D), jnp.float32)
    c = nrm(ks[1], (BATCH, D), jnp.float32)
    offset = jax.random.randint(ks[2], (BATCH, 1), 0, 1024, dtype=jnp.int32)
    positions = offset + jnp.arange(SEQ, dtype=jnp.int32)[None, :]
    return {
        'x': x, 'c': c, 'positions': positions,
        'w_ada': nrm(ks[3], (L, D, 6 * D), jnp.float32) * D ** -0.5,
        'b_ada': 0.02 * nrm(ks[4], (L, 6 * D), jnp.float32),
        'g_attn_pre': gain(ks[5], D),
        'g_attn_post': gain(ks[6], D),
        'w_in': nrm(ks[7], (L, D, IN_WIDTH), jnp.float32) * D ** -0.5,
        'g_q_norm': gain(ks[8], MLA_Q_RANK),
        'w_uq': nrm(ks[9], (L, MLA_Q_RANK, MLA_HEADS * (MLA_NOPE_DIM + MLA_ROPE_DIM)), jnp.float32) * MLA_Q_RANK ** -0.5,
        'g_kv_norm': gain(ks[10], MLA_KV_RANK),
        'w_ukv': nrm(ks[11], (L, MLA_KV_RANK, MLA_HEADS * (MLA_NOPE_DIM + MLA_V_DIM)), jnp.float32) * MLA_KV_RANK ** -0.5,
        'b_forget': 3.0 + 0.5 * nrm(ks[12], (L, FOX_HEADS), jnp.float32),
        'w_out': nrm(ks[13], (L, MIX_WIDTH, D), jnp.float32) * MIX_WIDTH ** -0.5,
        'g_moe_pre': gain(ks[14], D),
        'g_moe_post': gain(ks[15], D),
        'w_router': nrm(ks[16], (L, D, E), jnp.float32) * D ** -0.5,
        'b_router': 0.01 * nrm(ks[17], (L, E), jnp.float32),
        'w_gate_up': nrm(ks[18], (L, E, D, 2 * F), jnp.float32) * D ** -0.5,
        'b_gate_up': 0.02 * nrm(ks[19], (L, E, 2 * F), jnp.float32),
        'w_down': nrm(ks[20], (L, E, F, D), jnp.float32) * F ** -0.5,
        'b_down': 0.02 * nrm(ks[21], (L, E, D), jnp.float32),
    }


def reference(x, c, positions, w_ada, b_ada, g_attn_pre, g_attn_post, w_in, g_q_norm, w_uq,
              g_kv_norm, w_ukv, b_forget, w_out, g_moe_pre, g_moe_post, w_router, b_router,
              w_gate_up, b_gate_up, w_down, b_down):
    cos, sin = rope_tables(positions)
    c_act = jax.nn.silu(c)
    for layer in range(DEPTH):
        ada = c_act @ w_ada[layer] + b_ada[layer]
        sh_a, sc_a, gt_a, sh_m, sc_m, gt_m = jnp.split(ada, 6, axis=-1)
        x = mixer_sublayer(x, sh_a, sc_a, gt_a, cos, sin, g_attn_pre[layer], g_attn_post[layer],
                           w_in[layer], g_q_norm[layer], w_uq[layer], g_kv_norm[layer],
                           w_ukv[layer], b_forget[layer], w_out[layer])
        x = moe_sublayer(x, sh_m, sc_m, gt_m, g_moe_pre[layer], g_moe_post[layer],
                         w_router[layer], b_router[layer], w_gate_up[layer], b_gate_up[layer],
                         w_down[layer], b_down[layer])
    return x
```

```python
import functools
import math

import jax
import jax.numpy as jnp
from jax import lax
from jax.experimental import pallas as pl
from jax.experimental.pallas import tpu as pltpu

F32 = jnp.float32
BF16 = jnp.bfloat16

MLA_HEADS = 8
NOPE_DIM = 64
ROPE_DIM = 32
V_DIM = 64
Q_RANK = 256
KV_RANK = 128
FOX_HEADS = 8
FOX_DIM = 64
FOX_WIDTH = FOX_HEADS * FOX_DIM
ROPE_THETA = 10000.0
N_EXPERTS = 32
TOP_K = 4
SWIGLU_ALPHA = 1.702
SWIGLU_LIMIT = 7.0
RMS_EPS = 1e-6
NEG_INF = -1e30
MLA_SCALE = 1.0 / math.sqrt(NOPE_DIM + ROPE_DIM)
FOX_SCALE = 1.0 / math.sqrt(FOX_DIM)

LANES = 128
C_CQ = (0, 256)
C_CKV = (256, 384)
C_KR = (384, 640)
C_FQ = (640, 1152)
C_FK = (1152, 1664)
C_FV = (1664, 2176)
W1_COLS = 2176
FL_ROWS = 16

VMEM_LIMIT = 56 * 1024 * 1024


def _cparams(sem):
    return pltpu.CompilerParams(dimension_semantics=sem, vmem_limit_bytes=VMEM_LIMIT)


def _rms(x, g):
    return x * lax.rsqrt(jnp.mean(x * x, axis=-1, keepdims=True) + RMS_EPS) * g


def _dot(a, b):
    return jnp.dot(a, b, preferred_element_type=F32)


def _dot_nt(a, b):
    return lax.dot_general(a, b, (((1,), (1,)), ((), ())), preferred_element_type=F32)


def _ada_kernel(c_ref, w_ref, b_ref, o_ref):
    c = c_ref[...]
    ca = (c * jax.nn.sigmoid(c)).astype(BF16)
    o_ref[...] = _dot(ca, w_ref[...].astype(BF16)) + b_ref[...]


def _ada(c, w, b):
    bsz, d = c.shape
    n = w.shape[1]
    bn = 1024
    return pl.pallas_call(
        _ada_kernel,
        grid=(n // bn,),
        in_specs=[
            pl.BlockSpec((bsz, d), lambda j: (0, 0)),
            pl.BlockSpec((d, bn), lambda j: (0, j)),
            pl.BlockSpec((1, bn), lambda j: (0, j)),
        ],
        out_specs=pl.BlockSpec((bsz, bn), lambda j: (0, j)),
        out_shape=jax.ShapeDtypeStruct((bsz, n), F32),
        compiler_params=_cparams(("arbitrary",)),
        name="ada",
    )(c, w, b.reshape(1, n))


def _pre_attn_kernel(x_ref, sc_ref, sh_ref, gpre_ref, w1_ref, wfl_ref, bf_ref, gq_ref, wqa_ref,
                     wqb_ref, gkv_ref, wuk_ref, wuv_ref, cp_ref, sp_ref,
                     q_ref, k_ref, v_ref, fq_ref, fk_ref, fv_ref, fc_ref, carry_ref):
    si = pl.program_id(1)
    ts = x_ref.shape[1]
    x = x_ref[0]
    h = _rms(x, gpre_ref[...]) * (1.0 + sc_ref[0]) + sh_ref[0]
    hb = h.astype(BF16)

    def proj(c):
        return _dot(hb, w1_ref[:, c[0]:c[1]])

    cp = cp_ref[0]
    sp = sp_ref[0]

    cqn = _rms(proj(C_CQ), gq_ref[...]).astype(BF16)
    qa = _dot(cqn, wqa_ref[...])
    qb = _dot(cqn, wqb_ref[...])
    for hd in range(MLA_HEADS):
        sl = slice(hd * LANES, (hd + 1) * LANES)
        q_ref[0, :, sl] = ((qa[:, sl] * cp + qb[:, sl] * sp) * MLA_SCALE).astype(BF16)

    kr = proj(C_KR)
    k_rope = kr[:, :LANES] * cp + kr[:, LANES:] * sp
    ckvn = _rms(proj(C_CKV), gkv_ref[...]).astype(BF16)
    kn = _dot(ckvn, wuk_ref[...])
    for hd in range(MLA_HEADS):
        sl = slice(hd * LANES, (hd + 1) * LANES)
        k_ref[0, :, sl] = (kn[:, sl] + k_rope).astype(BF16)
    v_ref[0] = _dot(ckvn, wuv_ref[...]).astype(BF16)

    fq_ref[0] = (proj(C_FQ) * FOX_SCALE).astype(BF16)
    fk_ref[0] = proj(C_FK).astype(BF16)
    fv_ref[0] = proj(C_FV).astype(BF16)

    fl = _dot_nt(wfl_ref[...], hb) + bf_ref[...]
    lf = jnp.minimum(fl, 0.0) - jnp.log1p(jnp.exp(-jnp.abs(fl)))
    r = lax.broadcasted_iota(jnp.int32, (ts, ts), 0)
    c = lax.broadcasted_iota(jnp.int32, (ts, ts), 1)
    tri = (r <= c).astype(BF16)
    p0 = lf.astype(BF16)
    r1 = lf - p0.astype(F32)
    p1 = r1.astype(BF16)
    p2 = (r1 - p1.astype(F32)).astype(BF16)
    cs = _dot(p0, tri) + _dot(p1, tri) + _dot(p2, tri)

    @pl.when(si == 0)
    def _():
        carry_ref[...] = jnp.zeros_like(carry_ref)

    cs = cs + carry_ref[:, 0:1]
    fc_ref[0] = cs
    carry_ref[...] = jnp.broadcast_to(cs[:, ts - 1:ts], carry_ref.shape)


def _pre_attn(x, sc, sh, g_pre, w1, wfl, bfg, g_q, wqa, wqb, g_kv, wuk, wuv, cp, sp, ts):
    bsz, s, d = x.shape
    grid = (bsz, s // ts)
    tok = lambda width: pl.BlockSpec((1, ts, width), lambda b, i: (b, i, 0))
    per_b = pl.BlockSpec((1, 1, d), lambda b, i: (b, 0, 0))
    full = lambda a: pl.BlockSpec(a.shape, lambda b, i: (0,) * a.ndim)
    outs = [
        jax.ShapeDtypeStruct((bsz, s, MLA_HEADS * LANES), BF16),
        jax.ShapeDtypeStruct((bsz, s, MLA_HEADS * LANES), BF16),
        jax.ShapeDtypeStruct((bsz, s, MLA_HEADS * V_DIM), BF16),
        jax.ShapeDtypeStruct((bsz, s, FOX_WIDTH), BF16),
        jax.ShapeDtypeStruct((bsz, s, FOX_WIDTH), BF16),
        jax.ShapeDtypeStruct((bsz, s, FOX_WIDTH), BF16),
        jax.ShapeDtypeStruct((bsz, FL_ROWS, s), F32),
    ]
    out_specs = [tok(MLA_HEADS * LANES), tok(MLA_HEADS * LANES), tok(MLA_HEADS * V_DIM),
                 tok(FOX_WIDTH), tok(FOX_WIDTH), tok(FOX_WIDTH),
                 pl.BlockSpec((1, FL_ROWS, ts), lambda b, i: (b, 0, i))]
    return pl.pallas_call(
        _pre_attn_kernel,
        grid=grid,
        in_specs=[tok(d), per_b, per_b, full(g_pre), full(w1), full(wfl), full(bfg), full(g_q),
                  full(wqa), full(wqb), full(g_kv), full(wuk), full(wuv), tok(LANES), tok(LANES)],
        out_specs=out_specs,
        out_shape=outs,
        scratch_shapes=[pltpu.VMEM((FL_ROWS, LANES), F32)],
        compiler_params=_cparams(("arbitrary", "arbitrary")),
        name="pre_attn",
    )(x, sc, sh, g_pre, w1, wfl, bfg, g_q, wqa, wqb, g_kv, wuk, wuv, cp, sp)


def _attn_kernel(*refs, tq, head_lanes, has_bias):
    if has_bias:
        q_ref, k_ref, v_ref, f_ref, o_ref, acc_ref = refs
    else:
        q_ref, k_ref, v_ref, o_ref, acc_ref = refs
        f_ref = None
    qi = pl.program_id(2)
    lane = lax.broadcasted_iota(jnp.int32, (tq, LANES), 1)
    if head_lanes == LANES:
        qs = [q_ref[0, :, 0:LANES], q_ref[0, :, LANES:2 * LANES]]
    else:
        q2 = q_ref[0]
        zero = jnp.zeros_like(q2)
        qs = [jnp.where(lane < head_lanes, q2, zero), jnp.where(lane >= head_lanes, q2, zero)]

    acc_ref[...] = jnp.zeros_like(acc_ref)

    def step(j, carry, masked):
        off = pl.multiple_of(j * tq, tq)
        v2 = v_ref[0, pl.ds(off, tq), :]
        new = []
        for hd in range(2):
            m, l = carry[2 * hd], carry[2 * hd + 1]
            if head_lanes == LANES:
                kk = k_ref[0, pl.ds(off, tq), hd * LANES:(hd + 1) * LANES]
            else:
                kk = k_ref[0, pl.ds(off, tq), :]
            s = _dot_nt(qs[hd], kk)
            if has_bias:
                s = s - f_ref[0, 0, hd:hd + 1, pl.ds(off, tq)]
            if masked:
                rr = lax.broadcasted_iota(jnp.int32, (tq, tq), 0)
                cc = lax.broadcasted_iota(jnp.int32, (tq, tq), 1)
                s = jnp.where(cc <= rr, s, NEG_INF)
            m_new = jnp.maximum(m, jnp.max(s, axis=-1, keepdims=True))
            alpha = jnp.exp(m - m_new)
            p = jnp.exp(s - m_new)
            l_new = alpha * l + jnp.sum(p, axis=-1, keepdims=True)
            acc_ref[hd] = alpha * acc_ref[hd] + _dot(p.astype(BF16), v2)
            new += [m_new, l_new]
        return tuple(new)

    init = (jnp.full((tq, 1), NEG_INF, F32), jnp.zeros((tq, 1), F32)) * 2
    carry = lax.fori_loop(0, qi, lambda j, c: step(j, c, False), init)
    carry = step(qi, carry, True)
    o0 = acc_ref[0] / carry[1]
    o1 = acc_ref[1] / carry[3]
    o_ref[0] = jnp.where(lane < V_DIM, o0, o1).astype(o_ref.dtype)


def _attention(q, k, v, fcum, tq, head_lanes):
    bsz, s, _ = q.shape
    n_pairs = v.shape[2] // LANES
    qk_w = 2 * head_lanes
    has_bias = fcum is not None
    in_specs = [
        pl.BlockSpec((1, tq, qk_w), lambda b, p, i: (b, i, p)),
        pl.BlockSpec((1, s, qk_w), lambda b, p, i: (b, 0, p)),
        pl.BlockSpec((1, s, LANES), lambda b, p, i: (b, 0, p)),
    ]
    args = [q, k, v]
    if has_bias:
        in_specs.append(pl.BlockSpec((1, 1, 2, s), lambda b, p, i: (b, p, 0, 0)))
        args.append(fcum)
    return pl.pallas_call(
        functools.partial(_attn_kernel, tq=tq, head_lanes=head_lanes, has_bias=has_bias),
        grid=(bsz, n_pairs, s // tq),
        in_specs=in_specs,
        out_specs=pl.BlockSpec((1, tq, LANES), lambda b, p, i: (b, i, p)),
        out_shape=jax.ShapeDtypeStruct((bsz, s, n_pairs * LANES), BF16),
        scratch_shapes=[pltpu.VMEM((2, tq, LANES), F32)],
        compiler_params=_cparams(("arbitrary", "arbitrary", "arbitrary")),
        name="attn_fox" if has_bias else "attn_mla",
    )(*args)


def _post_attn_kernel(om_ref, of_ref, x_ref, wo_ref, gpost_ref, gate_ref, gmoe_ref, sc_ref, sh_ref,
                      wr_ref, br_ref, x1_ref, h2_ref, idx_ref, gts_ref):
    half = om_ref.shape[2]
    o = _dot(om_ref[0], wo_ref[0:half, :]) + _dot(of_ref[0], wo_ref[half:, :])
    x1 = x_ref[0] + gate_ref[0] * _rms(o, gpost_ref[...])
    x1_ref[0] = x1
    h2 = _rms(x1, gmoe_ref[...]) * (1.0 + sc_ref[0]) + sh_ref[0]
    h2_ref[0] = h2
    logits = _dot_nt(wr_ref[...], h2.astype(BF16)) + br_ref[...]
    n_e = logits.shape[0]
    eid = lax.broadcasted_iota(jnp.int32, logits.shape, 0)
    vals, idxs = [], []
    for _ in range(TOP_K):
        m = jnp.max(logits, axis=0, keepdims=True)
        ix = jnp.min(jnp.where(logits == m, eid, n_e), axis=0, keepdims=True)
        vals.append(m)
        idxs.append(ix)
        logits = jnp.where(eid == ix, -jnp.inf, logits)
    es = [jnp.exp(vv - vals[0]) for vv in vals]
    den = es[0] + es[1] + es[2] + es[3]
    for kk in range(TOP_K):
        idx_ref[0, kk:kk + 1, :] = idxs[kk]
        gts_ref[0, kk:kk + 1, :] = es[kk] / den


def _post_attn(o_mla, o_fox, x, wo, g_post, gate, g_moe, sc, sh, wr_t, br, ts):
    bsz, s, d = x.shape
    half = o_mla.shape[2]
    tok = lambda width: pl.BlockSpec((1, ts, width), lambda b, i: (b, i, 0))
    per_b = pl.BlockSpec((1, 1, d), lambda b, i: (b, 0, 0))
    full = lambda a: pl.BlockSpec(a.shape, lambda b, i: (0,) * a.ndim)
    k_spec = pl.BlockSpec((1, TOP_K, ts), lambda b, i: (b, 0, i))
    return pl.pallas_call(
        _post_attn_kernel,
        grid=(bsz, s // ts),
        in_specs=[tok(half), tok(half), tok(d), full(wo), full(g_post), per_b, full(g_moe), per_b,
                  per_b, full(wr_t), full(br)],
        out_specs=[tok(d), tok(d), k_spec, k_spec],
        out_shape=[jax.ShapeDtypeStruct((bsz, s, d), F32), jax.ShapeDtypeStruct((bsz, s, d), F32),
                   jax.ShapeDtypeStruct((bsz, TOP_K, s), jnp.int32),
                   jax.ShapeDtypeStruct((bsz, TOP_K, s), F32)],
        compiler_params=_cparams(("arbitrary", "arbitrary")),
        name="post_attn",
    )(o_mla, o_fox, x, wo, g_post, gate, g_moe, sc, sh, wr_t, br)


def _gather_rows(idx_ref, src_hbm, dst_ref, sem, n_rows, base):
    def issue(r, _):
        pltpu.make_async_copy(src_hbm.at[pl.ds(idx_ref[r], 1), :],
                              dst_ref.at[pl.ds(base + r, 1), :], sem).start()
        return 0
    lax.fori_loop(0, n_rows, issue, 0, unroll=8)

    def drain(r, _):
        pltpu.make_async_copy(src_hbm.at[pl.ds(0, 1), :],
                              dst_ref.at[pl.ds(base + r, 1), :], sem).wait()
        return 0
    lax.fori_loop(0, n_rows, drain, 0, unroll=8)


def _dispatch_kernel(nused_ref, src_ref, h_hbm, o_ref, sem):
    i = pl.program_id(0)
    bm = o_ref.shape[0]

    @pl.when(i < nused_ref[0])
    def _():
        _gather_rows(src_ref.at[0, 0], h_hbm, o_ref, sem, bm, 0)

    @pl.when(i >= nused_ref[0])
    def _():
        o_ref[...] = jnp.zeros_like(o_ref)


def _dispatch(n_used, src_tok, h2, bm):
    t, d = h2.shape
    n_blocks = src_tok.shape[0] // bm
    return pl.pallas_call(
        _dispatch_kernel,
        grid_spec=pltpu.PrefetchScalarGridSpec(
            num_scalar_prefetch=1,
            grid=(n_blocks,),
            in_specs=[
                pl.BlockSpec((1, 1, bm), lambda i, nu: (i, 0, 0), memory_space=pltpu.SMEM),
                pl.BlockSpec(memory_space=pl.ANY),
            ],
            out_specs=pl.BlockSpec((bm, d), lambda i, nu: (i, 0)),
            scratch_shapes=[pltpu.SemaphoreType.DMA],
        ),
        out_shape=jax.ShapeDtypeStruct((n_blocks * bm, d), h2.dtype),
        compiler_params=_cparams(("arbitrary",)),
        name="dispatch",
    )(n_used, src_tok.reshape(n_blocks, 1, bm), h2)


def _expert_kernel(be_ref, first_ref, nused_ref, x_ref, wgu_ref, bgu_ref, wd_ref, bd_ref, y_ref,
                   wgu_bf, wd_bf):
    i = pl.program_id(0)
    ff = wd_ref.shape[1]

    @pl.when(first_ref[i] == 1)
    def _():
        wgu_bf[...] = wgu_ref[0].astype(BF16)
        wd_bf[...] = wd_ref[0].astype(BF16)

    @pl.when(i < nused_ref[0])
    def _():
        xb = x_ref[...].astype(BF16)
        gu = _dot(xb, wgu_bf[...]) + bgu_ref[0]
        g = jnp.minimum(gu[:, :ff], SWIGLU_LIMIT)
        u = jnp.clip(gu[:, ff:], -SWIGLU_LIMIT, SWIGLU_LIMIT)
        glu = g * jax.nn.sigmoid(SWIGLU_ALPHA * g)
        act = ((u + 1.0) * glu).astype(BF16)
        y_ref[...] = _dot(act, wd_bf[...]) + bd_ref[0]

    @pl.when(i >= nused_ref[0])
    def _():
        y_ref[...] = jnp.zeros_like(y_ref)


def _experts(block_expert, first, n_used, xs, w_gu, b_gu, w_d, b_d, bm):
    p, d = xs.shape
    e, _, f2 = w_gu.shape
    ff = w_d.shape[1]
    n_blocks = p // bm
    return pl.pallas_call(
        _expert_kernel,
        grid_spec=pltpu.PrefetchScalarGridSpec(
            num_scalar_prefetch=3,
            grid=(n_blocks,),
            in_specs=[
                pl.BlockSpec((bm, d), lambda i, be, fi, nu: (i, 0)),
                pl.BlockSpec((1, d, f2), lambda i, be, fi, nu: (be[i], 0, 0)),
                pl.BlockSpec((1, 1, f2), lambda i, be, fi, nu: (be[i], 0, 0)),
                pl.BlockSpec((1, ff, d), lambda i, be, fi, nu: (be[i], 0, 0)),
                pl.BlockSpec((1, 1, d), lambda i, be, fi, nu: (be[i], 0, 0)),
            ],
            out_specs=pl.BlockSpec((bm, d), lambda i, be, fi, nu: (i, 0)),
            scratch_shapes=[pltpu.VMEM((d, f2), BF16), pltpu.VMEM((ff, d), BF16)],
        ),
        out_shape=jax.ShapeDtypeStruct((p, d), F32),
        compiler_params=_cparams(("arbitrary",)),
        name="experts",
    )(block_expert, first, n_used, xs, w_gu, b_gu.reshape(e, 1, f2), w_d, b_d.reshape(e, 1, d))


def _combine_kernel(dest_ref, y_hbm, gts_ref, x1_ref, gpost_ref, gate_ref, o_ref, buf, sem):
    tt = o_ref.shape[1]
    _gather_rows(dest_ref.at[0, 0], y_hbm, buf, sem, TOP_K * tt, 0)
    gts = gts_ref[0]
    acc = gts[:, 0:1] * buf[0:tt, :]
    for kk in range(1, TOP_K):
        acc = acc + gts[:, kk:kk + 1] * buf[kk * tt:(kk + 1) * tt, :]
    o_ref[0] = x1_ref[0] + gate_ref[0] * _rms(acc, gpost_ref[...])


def _combine(dest_kt, y_sorted, gts, x1, g_post, gate, tt):
    bsz, s, d = x1.shape
    n_t = s // tt
    return pl.pallas_call(
        _combine_kernel,
        grid=(bsz, n_t),
        in_specs=[
            pl.BlockSpec((1, 1, TOP_K * tt), lambda b, i: (b * n_t + i, 0, 0),
                         memory_space=pltpu.SMEM),
            pl.BlockSpec(memory_space=pl.ANY),
            pl.BlockSpec((1, tt, TOP_K), lambda b, i: (b, i, 0)),
            pl.BlockSpec((1, tt, d), lambda b, i: (b, i, 0)),
            pl.BlockSpec(g_post.shape, lambda b, i: (0, 0)),
            pl.BlockSpec((1, 1, d), lambda b, i: (b, 0, 0)),
        ],
        out_specs=pl.BlockSpec((1, tt, d), lambda b, i: (b, i, 0)),
        out_shape=jax.ShapeDtypeStruct((bsz, s, d), F32),
        scratch_shapes=[pltpu.VMEM((TOP_K * tt, d), F32), pltpu.SemaphoreType.DMA],
        compiler_params=_cparams(("arbitrary", "arbitrary")),
        name="combine",
    )(dest_kt, y_sorted, gts, x1, g_post, gate)


def _rope_patterns(positions):
    inv_freq = ROPE_THETA ** (-jnp.arange(0, ROPE_DIM, 2, dtype=F32) / ROPE_DIM)
    ang = positions.astype(F32)[..., None] * inv_freq
    cos, sin = jnp.cos(ang), jnp.sin(ang)
    ones = jnp.ones(positions.shape + (NOPE_DIM,), F32)
    zeros = jnp.zeros(positions.shape + (LANES - NOPE_DIM - ROPE_DIM,), F32)
    cp = jnp.concatenate([ones, cos, cos, zeros], axis=-1)
    sp = jnp.concatenate([0.0 * ones, -sin, sin, zeros], axis=-1)
    return cp, sp


def _prep_mixer_weights(w_in, w_uq, w_ukv, b_forget):
    d = w_in.shape[0]
    o_kr = Q_RANK + KV_RANK
    o_f = o_kr + ROPE_DIM
    hr = ROPE_DIM // 2
    z = lambda n: jnp.zeros((d, n), w_in.dtype)
    kr = w_in[:, o_kr:o_f]
    kr_sw = jnp.concatenate([kr[:, hr:], kr[:, :hr]], axis=1)
    pad = LANES - NOPE_DIM - ROPE_DIM
    w1 = jnp.concatenate([
        w_in[:, :o_kr],
        z(NOPE_DIM), kr, z(pad),
        z(NOPE_DIM), kr_sw, z(pad),
        w_in[:, o_f:o_f + 3 * FOX_WIDTH],
    ], axis=1).astype(BF16)
    wfl = jnp.zeros((FL_ROWS, d), w_in.dtype).at[:FOX_HEADS].set(
        w_in[:, o_f + 3 * FOX_WIDTH:].T).astype(BF16)
    bfg = jnp.zeros((FL_ROWS, 1), F32).at[:FOX_HEADS, 0].set(b_forget)

    wq = w_uq.reshape(Q_RANK, MLA_HEADS, NOPE_DIM + ROPE_DIM)
    nope, rope = wq[..., :NOPE_DIM], wq[..., NOPE_DIM:]
    zq = lambda n: jnp.zeros((Q_RANK, MLA_HEADS, n), w_uq.dtype)
    wqa = jnp.concatenate([nope, rope, zq(pad)], axis=-1).reshape(Q_RANK, -1).astype(BF16)
    wqb = jnp.concatenate([zq(NOPE_DIM), rope[..., hr:], rope[..., :hr], zq(pad)],
                          axis=-1).reshape(Q_RANK, -1).astype(BF16)
    wkv = w_ukv.reshape(KV_RANK, MLA_HEADS, NOPE_DIM + V_DIM)
    wuk = jnp.concatenate([wkv[..., :NOPE_DIM],
                           jnp.zeros((KV_RANK, MLA_HEADS, LANES - NOPE_DIM), w_ukv.dtype)],
                          axis=-1).reshape(KV_RANK, -1).astype(BF16)
    wuv = wkv[..., NOPE_DIM:].reshape(KV_RANK, -1).astype(BF16)
    return w1, wfl, bfg, wqa, wqb, wuk, wuv


def _routing_tables(idx_kt, bm):
    bsz, _, s = idx_kt.shape
    t = bsz * s
    a = t * TOP_K
    idx = idx_kt.transpose(0, 2, 1).reshape(t, TOP_K)
    multi_hot = jnp.sum((idx[:, :, None] == jnp.arange(N_EXPERTS, dtype=jnp.int32)).astype(jnp.int32),
                        axis=1)
    incl = jnp.cumsum(multi_hot, axis=0)
    rank = jnp.take_along_axis(incl - multi_hot, idx, axis=1)
    counts = incl[-1]
    pcounts = ((counts + bm - 1) // bm) * bm
    pends = jnp.cumsum(pcounts)
    pstarts = pends - pcounts
    dest = pstarts[idx] + rank
    n_blocks = a // bm + N_EXPERTS
    n_used = (pends[-1] // bm).astype(jnp.int32).reshape(1)
    blk = jnp.arange(n_blocks, dtype=jnp.int32)
    block_expert = jnp.minimum(jnp.searchsorted(pends, blk * bm, side='right'),
                               N_EXPERTS - 1).astype(jnp.int32)
    first = jnp.concatenate([jnp.ones((1,), jnp.int32),
                             (block_expert[1:] != block_expert[:-1]).astype(jnp.int32)])
    tok = jnp.broadcast_to(jnp.arange(t, dtype=jnp.int32)[:, None], (t, TOP_K))
    src_tok = jnp.zeros((n_blocks * bm,), jnp.int32).at[dest.reshape(-1)].set(
        tok.reshape(-1), unique_indices=True)
    return dest, src_tok, block_expert, first, n_used


def _layer(x, ada, cp, sp, g_attn_pre, g_attn_post, w_in, g_q_norm, w_uq, g_kv_norm, w_ukv,
           b_forget, w_out, g_moe_pre, g_moe_post, w_router, b_router, w_gate_up, b_gate_up,
           w_down, b_down):
    bsz, s, d = x.shape
    ts = min(512, s)
    tq = min(512, s)
    tt = min(256, s)
    bm = 512
    row = lambda v: v.reshape(1, -1)
    sh_a, sc_a, gt_a, sh_m, sc_m, gt_m = [v.reshape(bsz, 1, d) for v in jnp.split(ada, 6, axis=-1)]

    w1, wfl, bfg, wqa, wqb, wuk, wuv = _prep_mixer_weights(w_in, w_uq, w_ukv, b_forget)
    q, k, v, fq, fk, fv, fcum = _pre_attn(x, sc_a, sh_a, row(g_attn_pre), w1, wfl, bfg,
                                          row(g_q_norm), wqa, wqb, row(g_kv_norm), wuk, wuv,
                                          cp, sp, ts)
    o_mla = _attention(q, k, v, None, tq, LANES)
    fpairs = fcum[:, :FOX_HEADS].reshape(bsz, FOX_HEADS // 2, 2, s)
    o_fox = _attention(fq, fk, fv, fpairs, tq, FOX_DIM)

    x1, h2, idx_kt, gts_kt = _post_attn(o_mla, o_fox, x, w_out.astype(BF16), row(g_attn_post), gt_a,
                                        row(g_moe_pre), sc_m, sh_m, w_router.T.astype(BF16),
                                        b_router.reshape(-1, 1), ts)

    dest, src_tok, block_expert, first, n_used = _routing_tables(idx_kt, bm)
    xs = _dispatch(n_used, src_tok, h2.reshape(bsz * s, d), bm)
    ys = _experts(block_expert, first, n_used, xs, w_gate_up, b_gate_up, w_down, b_down, bm)
    n_t = s // tt
    dest_kt = dest.reshape(bsz * n_t, tt, TOP_K).transpose(0, 2, 1).reshape(bsz * n_t, 1, TOP_K * tt)
    gts = gts_kt.transpose(0, 2, 1)
    return _combine(dest_kt, ys, gts, x1, row(g_moe_post), gt_m, tt)


def kernel(x, c, positions, w_ada, b_ada, g_attn_pre, g_attn_post, w_in, g_q_norm, w_uq, g_kv_norm,
           w_ukv, b_forget, w_out, g_moe_pre, g_moe_post, w_router, b_router, w_gate_up, b_gate_up,
           w_down, b_down):
    cp, sp = _rope_patterns(positions)
    for layer in range(w_ada.shape[0]):
        ada = _ada(c, w_ada[layer], b_ada[layer])
        x = _layer(x, ada, cp, sp, g_attn_pre[layer], g_attn_post[layer], w_in[layer],
                   g_q_norm[layer], w_uq[layer], g_kv_norm[layer], w_ukv[layer], b_forget[layer],
                   w_out[layer], g_moe_pre[layer], g_moe_post[layer], w_router[layer],
                   b_router[layer], w_gate_up[layer], b_gate_up[layer], w_down[layer],
                   b_down[layer])
    return x
```

```python
import functools
import math

import jax
import jax.numpy as jnp
from jax import lax
from jax.experimental import pallas as pl
from jax.experimental.pallas import tpu as pltpu

F32 = jnp.float32
BF16 = jnp.bfloat16

MLA_HEADS = 8
NOPE_DIM = 64
ROPE_DIM = 32
V_DIM = 64
Q_RANK = 256
KV_RANK = 128
FOX_HEADS = 8
FOX_DIM = 64
FOX_WIDTH = FOX_HEADS * FOX_DIM
ROPE_THETA = 10000.0
N_EXPERTS = 32
TOP_K = 4
SWIGLU_ALPHA = 1.702
SWIGLU_LIMIT = 7.0
RMS_EPS = 1e-6
NEG_INF = -1e30
MLA_SCALE = 1.0 / math.sqrt(NOPE_DIM + ROPE_DIM)
FOX_SCALE = 1.0 / math.sqrt(FOX_DIM)

LANES = 128
C_CQ = (0, 256)
C_CKV = (256, 384)
C_KR = (384, 640)
C_FQ = (640, 1152)
C_FK = (1152, 1664)
C_FV = (1664, 2176)
W1_COLS = 2176
FL_ROWS = 16

VMEM_LIMIT = 56 * 1024 * 1024


def _cparams(sem):
    return pltpu.CompilerParams(dimension_semantics=sem, vmem_limit_bytes=VMEM_LIMIT)


def _rms(x, g):
    return x * lax.rsqrt(jnp.mean(x * x, axis=-1, keepdims=True) + RMS_EPS) * g


def _dot(a, b):
    return jnp.dot(a, b, preferred_element_type=F32)


def _dot_nt(a, b):
    return lax.dot_general(a, b, (((1,), (1,)), ((), ())), preferred_element_type=F32)


def _ada_kernel(c_ref, w_ref, b_ref, o_ref):
    c = c_ref[...]
    ca = (c * jax.nn.sigmoid(c)).astype(BF16)
    o_ref[...] = _dot(ca, w_ref[...].astype(BF16)) + b_ref[...]


def _ada(c, w, b):
    bsz, d = c.shape
    n = w.shape[1]
    bn = 1024
    return pl.pallas_call(
        _ada_kernel,
        grid=(n // bn,),
        in_specs=[
            pl.BlockSpec((bsz, d), lambda j: (0, 0)),
            pl.BlockSpec((d, bn), lambda j: (0, j)),
            pl.BlockSpec((1, bn), lambda j: (0, j)),
        ],
        out_specs=pl.BlockSpec((bsz, bn), lambda j: (0, j)),
        out_shape=jax.ShapeDtypeStruct((bsz, n), F32),
        compiler_params=_cparams(("arbitrary",)),
        name="ada",
    )(c, w, b.reshape(1, n))


def _pre_attn_kernel(x_ref, sc_ref, sh_ref, gpre_ref, w1_ref, wfl_ref, bf_ref, gq_ref, wqa_ref,
                     wqb_ref, gkv_ref, wuk_ref, wuv_ref, cp_ref, sp_ref,
                     q_ref, k_ref, v_ref, fq_ref, fk_ref, fv_ref, fc_ref, carry_ref):
    si = pl.program_id(1)
    ts = x_ref.shape[1]
    x = x_ref[0]
    h = _rms(x, gpre_ref[...]) * (1.0 + sc_ref[0]) + sh_ref[0]
    hb = h.astype(BF16)

    def proj(c):
        return _dot(hb, w1_ref[:, c[0]:c[1]])

    cp = cp_ref[0]
    sp = sp_ref[0]

    cqn = _rms(proj(C_CQ), gq_ref[...]).astype(BF16)
    qa = _dot(cqn, wqa_ref[...])
    qb = _dot(cqn, wqb_ref[...])
    for hd in range(MLA_HEADS):
        sl = slice(hd * LANES, (hd + 1) * LANES)
        q_ref[0, :, sl] = ((qa[:, sl] * cp + qb[:, sl] * sp) * MLA_SCALE).astype(BF16)

    kr = proj(C_KR)
    k_rope = kr[:, :LANES] * cp + kr[:, LANES:] * sp
    ckvn = _rms(proj(C_CKV), gkv_ref[...]).astype(BF16)
    kn = _dot(ckvn, wuk_ref[...])
    for hd in range(MLA_HEADS):
        sl = slice(hd * LANES, (hd + 1) * LANES)
        k_ref[0, :, sl] = (kn[:, sl] + k_rope).astype(BF16)
    v_ref[0] = _dot(ckvn, wuv_ref[...]).astype(BF16)

    fq_ref[0] = (proj(C_FQ) * FOX_SCALE).astype(BF16)
    fk_ref[0] = proj(C_FK).astype(BF16)
    fv_ref[0] = proj(C_FV).astype(BF16)

    fl = _dot_nt(wfl_ref[...], hb) + bf_ref[...]
    lf = jnp.minimum(fl, 0.0) - jnp.log1p(jnp.exp(-jnp.abs(fl)))
    r = lax.broadcasted_iota(jnp.int32, (ts, ts), 0)
    c = lax.broadcasted_iota(jnp.int32, (ts, ts), 1)
    tri = (r <= c).astype(BF16)
    p0 = lf.astype(BF16)
    r1 = lf - p0.astype(F32)
    p1 = r1.astype(BF16)
    p2 = (r1 - p1.astype(F32)).astype(BF16)
    cs = _dot(p0, tri) + _dot(p1, tri) + _dot(p2, tri)

    @pl.when(si == 0)
    def _():
        carry_ref[...] = jnp.zeros_like(carry_ref)

    cs = cs + carry_ref[:, 0:1]
    fc_ref[0] = cs
    carry_ref[...] = jnp.broadcast_to(cs[:, ts - 1:ts], carry_ref.shape)


def _pre_attn(x, sc, sh, g_pre, w1, wfl, bfg, g_q, wqa, wqb, g_kv, wuk, wuv, cp, sp, ts):
    bsz, s, d = x.shape
    grid = (bsz, s // ts)
    tok = lambda width: pl.BlockSpec((1, ts, width), lambda b, i: (b, i, 0))
    per_b = pl.BlockSpec((1, 1, d), lambda b, i: (b, 0, 0))
    full = lambda a: pl.BlockSpec(a.shape, lambda b, i: (0,) * a.ndim)
    outs = [
        jax.ShapeDtypeStruct((bsz, s, MLA_HEADS * LANES), BF16),
        jax.ShapeDtypeStruct((bsz, s, MLA_HEADS * LANES), BF16),
        jax.ShapeDtypeStruct((bsz, s, MLA_HEADS * V_DIM), BF16),
        jax.ShapeDtypeStruct((bsz, s, FOX_WIDTH), BF16),
        jax.ShapeDtypeStruct((bsz, s, FOX_WIDTH), BF16),
        jax.ShapeDtypeStruct((bsz, s, FOX_WIDTH), BF16),
        jax.ShapeDtypeStruct((bsz, FL_ROWS, s), F32),
    ]
    out_specs = [tok(MLA_HEADS * LANES), tok(MLA_HEADS * LANES), tok(MLA_HEADS * V_DIM),
                 tok(FOX_WIDTH), tok(FOX_WIDTH), tok(FOX_WIDTH),
                 pl.BlockSpec((1, FL_ROWS, ts), lambda b, i: (b, 0, i))]
    return pl.pallas_call(
        _pre_attn_kernel,
        grid=grid,
        in_specs=[tok(d), per_b, per_b, full(g_pre), full(w1), full(wfl), full(bfg), full(g_q),
                  full(wqa), full(wqb), full(g_kv), full(wuk), full(wuv), tok(LANES), tok(LANES)],
        out_specs=out_specs,
        out_shape=outs,
        scratch_shapes=[pltpu.VMEM((FL_ROWS, LANES), F32)],
        compiler_params=_cparams(("arbitrary", "arbitrary")),
        name="pre_attn",
    )(x, sc, sh, g_pre, w1, wfl, bfg, g_q, wqa, wqb, g_kv, wuk, wuv, cp, sp)


def _attn_kernel(*refs, tq, head_lanes, has_bias):
    if has_bias:
        q_ref, k_ref, v_ref, f_ref, o_ref, acc_ref = refs
    else:
        q_ref, k_ref, v_ref, o_ref, acc_ref = refs
        f_ref = None
    qi = pl.program_id(2)
    lane = lax.broadcasted_iota(jnp.int32, (tq, LANES), 1)
    if head_lanes == LANES:
        qs = [q_ref[0, :, 0:LANES], q_ref[0, :, LANES:2 * LANES]]
    else:
        q2 = q_ref[0]
        zero = jnp.zeros_like(q2)
        qs = [jnp.where(lane < head_lanes, q2, zero), jnp.where(lane >= head_lanes, q2, zero)]

    acc_ref[...] = jnp.zeros_like(acc_ref)

    def step(j, carry, masked):
        off = pl.multiple_of(j * tq, tq)
        v2 = v_ref[0, pl.ds(off, tq), :]
        new = []
        for hd in range(2):
            m, l = carry[2 * hd], carry[2 * hd + 1]
            if head_lanes == LANES:
                kk = k_ref[0, pl.ds(off, tq), hd * LANES:(hd + 1) * LANES]
            else:
                kk = k_ref[0, pl.ds(off, tq), :]
            s = _dot_nt(qs[hd], kk)
            if has_bias:
                s = s - f_ref[0, 0, hd:hd + 1, pl.ds(off, tq)]
            if masked:
                rr = lax.broadcasted_iota(jnp.int32, (tq, tq), 0)
                cc = lax.broadcasted_iota(jnp.int32, (tq, tq), 1)
                s = jnp.where(cc <= rr, s, NEG_INF)
            m_new = jnp.maximum(m, jnp.max(s, axis=-1, keepdims=True))
            alpha = jnp.exp(m - m_new)
            p = jnp.exp(s - m_new)
            l_new = alpha * l + jnp.sum(p, axis=-1, keepdims=True)
            acc_ref[hd] = alpha * acc_ref[hd] + _dot(p.astype(BF16), v2)
            new += [m_new, l_new]
        return tuple(new)

    init = (jnp.full((tq, 1), NEG_INF, F32), jnp.zeros((tq, 1), F32)) * 2
    carry = lax.fori_loop(0, qi, lambda j, c: step(j, c, False), init)
    carry = step(qi, carry, True)
    o0 = acc_ref[0] / carry[1]
    o1 = acc_ref[1] / carry[3]
    o_ref[0] = jnp.where(lane < V_DIM, o0, o1).astype(o_ref.dtype)


def _attention(q, k, v, fcum, tq, head_lanes):
    bsz, s, _ = q.shape
    n_pairs = v.shape[2] // LANES
    qk_w = 2 * head_lanes
    has_bias = fcum is not None
    in_specs = [
        pl.BlockSpec((1, tq, qk_w), lambda b, p, i: (b, i, p)),
        pl.BlockSpec((1, s, qk_w), lambda b, p, i: (b, 0, p)),
        pl.BlockSpec((1, s, LANES), lambda b, p, i: (b, 0, p)),
    ]
    args = [q, k, v]
    if has_bias:
        in_specs.append(pl.BlockSpec((1, 1, 2, s), lambda b, p, i: (b, p, 0, 0)))
        args.append(fcum)
    return pl.pallas_call(
        functools.partial(_attn_kernel, tq=tq, head_lanes=head_lanes, has_bias=has_bias),
        grid=(bsz, n_pairs, s // tq),
        in_specs=in_specs,
        out_specs=pl.BlockSpec((1, tq, LANES), lambda b, p, i: (b, i, p)),
        out_shape=jax.ShapeDtypeStruct((bsz, s, n_pairs * LANES), BF16),
        scratch_shapes=[pltpu.VMEM((2, tq, LANES), F32)],
        compiler_params=_cparams(("arbitrary", "arbitrary", "arbitrary")),
        name="attn_fox" if has_bias else "attn_mla",
    )(*args)


def _post_attn_kernel(om_ref, of_ref, x_ref, wo_ref, gpost_ref, gate_ref, gmoe_ref, sc_ref, sh_ref,
                      wr_ref, br_ref, x1_ref, h2_ref, idx_ref, gts_ref):
    half = om_ref.shape[2]
    o = _dot(om_ref[0], wo_ref[0:half, :]) + _dot(of_ref[0], wo_ref[half:, :])
    x1 = x_ref[0] + gate_ref[0] * _rms(o, gpost_ref[...])
    x1_ref[0] = x1
    h2 = _rms(x1, gmoe_ref[...]) * (1.0 + sc_ref[0]) + sh_ref[0]
    h2_ref[0] = h2
    logits = _dot_nt(wr_ref[...], h2.astype(BF16)) + br_ref[...]
    n_e = logits.shape[0]
    eid = lax.broadcasted_iota(jnp.int32, logits.shape, 0)
    vals, idxs = [], []
    for _ in range(TOP_K):
        m = jnp.max(logits, axis=0, keepdims=True)
        ix = jnp.min(jnp.where(logits == m, eid, n_e), axis=0, keepdims=True)
        vals.append(m)
        idxs.append(ix)
        logits = jnp.where(eid == ix, -jnp.inf, logits)
    es = [jnp.exp(vv - vals[0]) for vv in vals]
    den = es[0] + es[1] + es[2] + es[3]
    for kk in range(TOP_K):
        idx_ref[0, kk:kk + 1, :] = idxs[kk]
        gts_ref[0, kk:kk + 1, :] = es[kk] / den


def _post_attn(o_mla, o_fox, x, wo, g_post, gate, g_moe, sc, sh, wr_t, br, ts):
    bsz, s, d = x.shape
    half = o_mla.shape[2]
    tok = lambda width: pl.BlockSpec((1, ts, width), lambda b, i: (b, i, 0))
    per_b = pl.BlockSpec((1, 1, d), lambda b, i: (b, 0, 0))
    full = lambda a: pl.BlockSpec(a.shape, lambda b, i: (0,) * a.ndim)
    k_spec = pl.BlockSpec((1, TOP_K, ts), lambda b, i: (b, 0, i))
    return pl.pallas_call(
        _post_attn_kernel,
        grid=(bsz, s // ts),
        in_specs=[tok(half), tok(half), tok(d), full(wo), full(g_post), per_b, full(g_moe), per_b,
                  per_b, full(wr_t), full(br)],
        out_specs=[tok(d), tok(d), k_spec, k_spec],
        out_shape=[jax.ShapeDtypeStruct((bsz, s, d), F32), jax.ShapeDtypeStruct((bsz, s, d), F32),
                   jax.ShapeDtypeStruct((bsz, TOP_K, s), jnp.int32),
                   jax.ShapeDtypeStruct((bsz, TOP_K, s), F32)],
        compiler_params=_cparams(("arbitrary", "arbitrary")),
        name="post_attn",
    )(o_mla, o_fox, x, wo, g_post, gate, g_moe, sc, sh, wr_t, br)


def _route_kernel(idx_ref, tri_ref, rank_ref, cnt_ref, run_ref):
    first = jnp.logical_and(pl.program_id(0) == 0, pl.program_id(1) == 0)

    @pl.when(first)
    def _():
        run_ref[...] = jnp.zeros_like(run_ref)

    tr = idx_ref.shape[2]
    eid = lax.broadcasted_iota(jnp.int32, (N_EXPERTS, tr), 0)
    hot = [eid == idx_ref[0, kk:kk + 1, :] for kk in range(TOP_K)]
    multi = hot[0] | hot[1] | hot[2] | hot[3]
    mf = jnp.where(multi, 1.0, 0.0)
    incl = _dot(mf.astype(BF16), tri_ref[...]) + run_ref[:, 0:1]
    excl = incl - mf
    for kk in range(TOP_K):
        rank_ref[0, kk:kk + 1, :] = jnp.sum(jnp.where(hot[kk], excl, 0.0), axis=0,
                                            keepdims=True).astype(jnp.int32)
    run_ref[...] = jnp.broadcast_to(incl[:, tr - 1:tr], run_ref.shape)
    cnt_ref[...] = run_ref[...].astype(jnp.int32)


def _route(idx_kt, tr):
    bsz, _, s = idx_kt.shape
    tri = (jnp.arange(tr)[:, None] <= jnp.arange(tr)[None, :]).astype(BF16)
    k_spec = pl.BlockSpec((1, TOP_K, tr), lambda b, i: (b, 0, i))
    return pl.pallas_call(
        _route_kernel,
        grid=(bsz, s // tr),
        in_specs=[k_spec, pl.BlockSpec((tr, tr), lambda b, i: (0, 0))],
        out_specs=[k_spec, pl.BlockSpec((N_EXPERTS, LANES), lambda b, i: (0, 0))],
        out_shape=[jax.ShapeDtypeStruct((bsz, TOP_K, s), jnp.int32),
                   jax.ShapeDtypeStruct((N_EXPERTS, LANES), jnp.int32)],
        scratch_shapes=[pltpu.VMEM((N_EXPERTS, LANES), F32)],
        compiler_params=_cparams(("arbitrary", "arbitrary")),
        name="route",
    )(idx_kt, tri)


def _expert_kernel(be_ref, first_ref, nused_ref, gidx_ref, sidx_ref, h_hbm, wgu_ref, bgu_ref, wd_ref,
                   bd_ref, y_hbm, xbuf0, xbuf1, ybuf0, ybuf1, wgu_bf, wd_bf, sem_g, sem_s):
    i = pl.program_id(0)
    n_used = nused_ref[0]
    bm = xbuf0.shape[0]
    ff = wd_ref.shape[1]
    xbufs = (xbuf0, xbuf1)
    ybufs = (ybuf0, ybuf1)

    def gather_start(slot, r):
        pltpu.make_async_copy(h_hbm.at[pl.ds(gidx_ref[0, 0, r], 1), :],
                              xbufs[slot].at[pl.ds(r, 1), :], sem_g.at[slot]).start()

    def scatter_start(slot, r):
        pltpu.make_async_copy(ybufs[slot].at[pl.ds(r, 1), :],
                              y_hbm.at[pl.ds(sidx_ref[0, 0, r], 1), :], sem_s.at[slot]).start()

    def gather_wait(slot):
        pltpu.make_async_copy(h_hbm.at[pl.ds(0, bm), :], xbufs[slot], sem_g.at[slot]).wait()

    def scatter_wait(slot):
        pltpu.make_async_copy(ybufs[slot], y_hbm.at[pl.ds(0, bm), :], sem_s.at[slot]).wait()

    def rolled(start_fn, slot):
        def body(r, _):
            start_fn(slot, r)
            return 0
        lax.fori_loop(0, bm, body, 0, unroll=8)

    @pl.when(jnp.logical_and(first_ref[i] == 1, i <= n_used))
    def _():
        wgu_bf[...] = wgu_ref[0].astype(BF16)
        wd_bf[...] = wd_ref[0].astype(BF16)

    def step(slot_in):
        slot_cur = 1 - slot_in

        if slot_in == 0:
            @pl.when(i == 0)
            def _():
                ybuf0[...] = jnp.zeros_like(ybuf0)
                ybuf1[...] = jnp.zeros_like(ybuf1)
                rolled(gather_start, 0)

        @pl.when(jnp.logical_and(i >= 1, i <= n_used + 1))
        def _():
            gather_wait(slot_cur)

        @pl.when(jnp.logical_and(i >= 2, i <= n_used + 1))
        def _():
            scatter_wait(slot_cur)

        @pl.when(jnp.logical_and(i >= 1, i <= n_used))
        def _():
            xb = xbufs[slot_cur][...].astype(BF16)
            for r in range(bm):
                scatter_start(slot_in, r)
            for r in range(bm):
                gather_start(slot_in, r)
            gu = _dot(xb, wgu_bf[...]) + bgu_ref[0]
            g = jnp.minimum(gu[:, :ff], SWIGLU_LIMIT)
            u = jnp.clip(gu[:, ff:], -SWIGLU_LIMIT, SWIGLU_LIMIT)
            glu = g * jax.nn.sigmoid(SWIGLU_ALPHA * g)
            act = ((u + 1.0) * glu).astype(BF16)
            ybufs[slot_cur][...] = _dot(act, wd_bf[...]) + bd_ref[0]

        @pl.when(i == n_used + 1)
        def _():
            rolled(scatter_start, slot_in)
            scatter_wait(slot_in)

    @pl.when(i % 2 == 0)
    def _():
        step(0)

    @pl.when(i % 2 == 1)
    def _():
        step(1)


def _experts(be_step, first_step, n_used, gidx, sidx, h2, w_gu, b_gu, w_d, b_d, n_out_rows, bm):
    t, d = h2.shape
    e, _, f2 = w_gu.shape
    ff = w_d.shape[1]
    n_steps = gidx.shape[0]
    smem_blk = pl.BlockSpec((1, 1, bm), lambda i, be, fi, nu: (i, 0, 0), memory_space=pltpu.SMEM)
    return pl.pallas_call(
        _expert_kernel,
        grid_spec=pltpu.PrefetchScalarGridSpec(
            num_scalar_prefetch=3,
            grid=(n_steps,),
            in_specs=[
                smem_blk,
                smem_blk,
                pl.BlockSpec(memory_space=pl.ANY),
                pl.BlockSpec((1, d, f2), lambda i, be, fi, nu: (be[i], 0, 0)),
                pl.BlockSpec((1, 1, f2), lambda i, be, fi, nu: (be[i], 0, 0)),
                pl.BlockSpec((1, ff, d), lambda i, be, fi, nu: (be[i], 0, 0)),
                pl.BlockSpec((1, 1, d), lambda i, be, fi, nu: (be[i], 0, 0)),
            ],
            out_specs=pl.BlockSpec(memory_space=pl.ANY),
            scratch_shapes=[pltpu.VMEM((bm, d), F32), pltpu.VMEM((bm, d), F32),
                            pltpu.VMEM((bm, d), F32), pltpu.VMEM((bm, d), F32),
                            pltpu.VMEM((d, f2), BF16), pltpu.VMEM((ff, d), BF16),
                            pltpu.SemaphoreType.DMA((2,)), pltpu.SemaphoreType.DMA((2,))],
        ),
        out_shape=jax.ShapeDtypeStruct((n_out_rows, d), F32),
        compiler_params=_cparams(("arbitrary",)),
        name="experts",
    )(be_step, first_step, n_used, gidx, sidx, h2, w_gu, b_gu.reshape(e, 1, f2), w_d,
      b_d.reshape(e, 1, d))


def _combine_kernel(y0_ref, y1_ref, y2_ref, y3_ref, gts_ref, x1_ref, gpost_ref, gate_ref, o_ref):
    gts = gts_ref[...]
    acc = gts[:, 0:1] * y0_ref[...]
    for kk, y_ref in enumerate((y1_ref, y2_ref, y3_ref), start=1):
        acc = acc + gts[:, kk:kk + 1] * y_ref[...]
    o_ref[0] = x1_ref[0] + gate_ref[0] * _rms(acc, gpost_ref[...])


def _combine(y4, gts, x1, g_post, gate, tt):
    bsz, s, d = x1.shape
    n_t = s // tt
    n_tiles = bsz * n_t
    y_spec = lambda kk: pl.BlockSpec((tt, d), lambda b, i: (kk * n_tiles + b * n_t + i, 0))
    return pl.pallas_call(
        _combine_kernel,
        grid=(bsz, n_t),
        in_specs=[
            y_spec(0), y_spec(1), y_spec(2), y_spec(3),
            pl.BlockSpec((tt, TOP_K), lambda b, i: (b * n_t + i, 0)),
            pl.BlockSpec((1, tt, d), lambda b, i: (b, i, 0)),
            pl.BlockSpec(g_post.shape, lambda b, i: (0, 0)),
            pl.BlockSpec((1, 1, d), lambda b, i: (b, 0, 0)),
        ],
        out_specs=pl.BlockSpec((1, tt, d), lambda b, i: (b, i, 0)),
        out_shape=jax.ShapeDtypeStruct((bsz, s, d), F32),
        compiler_params=_cparams(("arbitrary", "arbitrary")),
        name="combine",
    )(y4, y4, y4, y4, gts, x1, g_post, gate)


def _rope_patterns(positions):
    inv_freq = ROPE_THETA ** (-jnp.arange(0, ROPE_DIM, 2, dtype=F32) / ROPE_DIM)
    ang = positions.astype(F32)[..., None] * inv_freq
    cos, sin = jnp.cos(ang), jnp.sin(ang)
    ones = jnp.ones(positions.shape + (NOPE_DIM,), F32)
    zeros = jnp.zeros(positions.shape + (LANES - NOPE_DIM - ROPE_DIM,), F32)
    cp = jnp.concatenate([ones, cos, cos, zeros], axis=-1)
    sp = jnp.concatenate([0.0 * ones, -sin, sin, zeros], axis=-1)
    return cp, sp


def _prep_mixer_weights(w_in, w_uq, w_ukv, b_forget):
    d = w_in.shape[0]
    o_kr = Q_RANK + KV_RANK
    o_f = o_kr + ROPE_DIM
    hr = ROPE_DIM // 2
    z = lambda n: jnp.zeros((d, n), w_in.dtype)
    kr = w_in[:, o_kr:o_f]
    kr_sw = jnp.concatenate([kr[:, hr:], kr[:, :hr]], axis=1)
    pad = LANES - NOPE_DIM - ROPE_DIM
    w1 = jnp.concatenate([
        w_in[:, :o_kr],
        z(NOPE_DIM), kr, z(pad),
        z(NOPE_DIM), kr_sw, z(pad),
        w_in[:, o_f:o_f + 3 * FOX_WIDTH],
    ], axis=1).astype(BF16)
    wfl = jnp.zeros((FL_ROWS, d), w_in.dtype).at[:FOX_HEADS].set(
        w_in[:, o_f + 3 * FOX_WIDTH:].T).astype(BF16)
    bfg = jnp.zeros((FL_ROWS, 1), F32).at[:FOX_HEADS, 0].set(b_forget)

    wq = w_uq.reshape(Q_RANK, MLA_HEADS, NOPE_DIM + ROPE_DIM)
    nope, rope = wq[..., :NOPE_DIM], wq[..., NOPE_DIM:]
    zq = lambda n: jnp.zeros((Q_RANK, MLA_HEADS, n), w_uq.dtype)
    wqa = jnp.concatenate([nope, rope, zq(pad)], axis=-1).reshape(Q_RANK, -1).astype(BF16)
    wqb = jnp.concatenate([zq(NOPE_DIM), rope[..., hr:], rope[..., :hr], zq(pad)],
                          axis=-1).reshape(Q_RANK, -1).astype(BF16)
    wkv = w_ukv.reshape(KV_RANK, MLA_HEADS, NOPE_DIM + V_DIM)
    wuk = jnp.concatenate([wkv[..., :NOPE_DIM],
                           jnp.zeros((KV_RANK, MLA_HEADS, LANES - NOPE_DIM), w_ukv.dtype)],
                          axis=-1).reshape(KV_RANK, -1).astype(BF16)
    wuv = wkv[..., NOPE_DIM:].reshape(KV_RANK, -1).astype(BF16)
    return w1, wfl, bfg, wqa, wqb, wuk, wuv


def _slot_tables(idx_kt, rank_kt, counts, bm):
    bsz, _, s = idx_kt.shape
    t = bsz * s
    n_blocks = (t * TOP_K) // bm + N_EXPERTS
    n_real = t * TOP_K
    i32 = jnp.int32
    eids = jnp.arange(N_EXPERTS, dtype=i32)

    pcounts = ((counts + bm - 1) // bm) * bm
    incl_mat = (eids[:, None] <= eids[None, :]).astype(i32)
    pends = pcounts @ incl_mat
    pstarts = pends - pcounts
    cstarts = counts @ incl_mat - counts
    total = pends[-1]
    n_used = (total // bm).astype(i32).reshape(1)

    onehot = idx_kt[..., None] == eids
    dest = jnp.sum(jnp.where(onehot, pstarts, 0), axis=-1) + rank_kt
    tok = jnp.arange(t, dtype=i32).reshape(bsz, 1, s)
    kk = jnp.arange(TOP_K, dtype=i32).reshape(1, TOP_K, 1)
    yrow_real = jnp.broadcast_to(kk * t + tok, dest.shape)
    _, rows_sorted = lax.sort((dest.reshape(-1), yrow_real.reshape(-1)), num_keys=1)

    blk0 = jnp.arange(n_blocks, dtype=i32)
    be_blk = jnp.minimum(jnp.sum((pends[None, :] <= (blk0 * bm)[:, None]).astype(i32), axis=1),
                         N_EXPERTS - 1)
    hot_b = be_blk[:, None] == eids[None, :]
    pick = lambda v: jnp.sum(jnp.where(hot_b, v[None, :], 0), axis=1)[:, None]
    local = (blk0 * bm)[:, None] + jnp.arange(bm, dtype=i32)[None, :] - pick(pstarts)
    valid = jnp.logical_and(local < pick(counts), (blk0 < n_used[0])[:, None])
    j = jnp.clip(pick(cstarts) + local, 0, n_real - 1)
    rows = jnp.take(rows_sorted, j.reshape(-1), axis=0).reshape(n_blocks, bm)
    spare = TOP_K * t + jnp.arange(bm, dtype=i32)[None, :]
    yrow = jnp.where(valid, rows, spare)
    src_tok = jnp.where(valid, rows % t, 0)

    zeros2 = jnp.zeros((2, bm), i32)
    spare2 = jnp.broadcast_to(spare, (2, bm))
    gidx = jnp.concatenate([src_tok, zeros2]).reshape(n_blocks + 2, 1, bm)
    sidx = jnp.concatenate([spare2, yrow]).reshape(n_blocks + 2, 1, bm)

    blk = jnp.arange(n_blocks + 2, dtype=i32) - 1
    be = jnp.sum((pends[None, :] <= (jnp.maximum(blk, 0) * bm)[:, None]).astype(i32), axis=1)
    be_step = jnp.minimum(be, N_EXPERTS - 1)
    first_step = jnp.concatenate([jnp.ones((2,), i32),
                                  (be_step[2:] != be_step[1:-1]).astype(i32)])
    return gidx, sidx, be_step, first_step, n_used


def _layer(x, ada, cp, sp, g_attn_pre, g_attn_post, w_in, g_q_norm, w_uq, g_kv_norm, w_ukv,
           b_forget, w_out, g_moe_pre, g_moe_post, w_router, b_router, w_gate_up, b_gate_up,
           w_down, b_down):
    bsz, s, d = x.shape
    t = bsz * s
    ts = min(512, s)
    tq = min(512, s)
    tt = min(512, s)
    tr = min(1024, s)
    bm = 512
    row = lambda v: v.reshape(1, -1)
    sh_a, sc_a, gt_a, sh_m, sc_m, gt_m = [v.reshape(bsz, 1, d) for v in jnp.split(ada, 6, axis=-1)]

    w1, wfl, bfg, wqa, wqb, wuk, wuv = _prep_mixer_weights(w_in, w_uq, w_ukv, b_forget)
    q, k, v, fq, fk, fv, fcum = _pre_attn(x, sc_a, sh_a, row(g_attn_pre), w1, wfl, bfg,
                                          row(g_q_norm), wqa, wqb, row(g_kv_norm), wuk, wuv,
                                          cp, sp, ts)
    o_mla = _attention(q, k, v, None, tq, LANES)
    fpairs = fcum[:, :FOX_HEADS].reshape(bsz, FOX_HEADS // 2, 2, s)
    o_fox = _attention(fq, fk, fv, fpairs, tq, FOX_DIM)

    x1, h2, idx_kt, gts_kt = _post_attn(o_mla, o_fox, x, w_out.astype(BF16), row(g_attn_post), gt_a,
                                        row(g_moe_pre), sc_m, sh_m, w_router.T.astype(BF16),
                                        b_router.reshape(-1, 1), ts)

    rank_kt, cnt = _route(idx_kt, tr)
    gidx, sidx, be_step, first_step, n_used = _slot_tables(idx_kt, rank_kt, cnt[:, 0], bm)
    y4 = _experts(be_step, first_step, n_used, gidx, sidx, h2.reshape(t, d), w_gate_up, b_gate_up,
                  w_down, b_down, TOP_K * t + bm, bm)
    gts = gts_kt.transpose(0, 2, 1).reshape(t, TOP_K)
    return _combine(y4, gts, x1, row(g_moe_post), gt_m, tt)


def kernel(x, c, positions, w_ada, b_ada, g_attn_pre, g_attn_post, w_in, g_q_norm, w_uq, g_kv_norm,
           w_ukv, b_forget, w_out, g_moe_pre, g_moe_post, w_router, b_router, w_gate_up, b_gate_up,
           w_down, b_down):
    cp, sp = _rope_patterns(positions)
    for layer in range(w_ada.shape[0]):
        ada = _ada(c, w_ada[layer], b_ada[layer])
        x = _layer(x, ada, cp, sp, g_attn_pre[layer], g_attn_post[layer], w_in[layer],
                   g_q_norm[layer], w_uq[layer], g_kv_norm[layer], w_ukv[layer], b_forget[layer],
                   w_out[layer], g_moe_pre[layer], g_moe_post[layer], w_router[layer],
                   b_router[layer], w_gate_up[layer], b_gate_up[layer], w_down[layer],
                   b_down[layer])
    return x
```

```python
import functools
import math

import jax
import jax.numpy as jnp
from jax import lax
from jax.experimental import pallas as pl
from jax.experimental.pallas import tpu as pltpu

F32 = jnp.float32
BF16 = jnp.bfloat16

MLA_HEADS = 8
NOPE_DIM = 64
ROPE_DIM = 32
V_DIM = 64
Q_RANK = 256
KV_RANK = 128
FOX_HEADS = 8
FOX_DIM = 64
FOX_WIDTH = FOX_HEADS * FOX_DIM
ROPE_THETA = 10000.0
N_EXPERTS = 32
TOP_K = 4
SWIGLU_ALPHA = 1.702
SWIGLU_LIMIT = 7.0
RMS_EPS = 1e-6
NEG_INF = -1e30
MLA_SCALE = 1.0 / math.sqrt(NOPE_DIM + ROPE_DIM)
FOX_SCALE = 1.0 / math.sqrt(FOX_DIM)

LANES = 128
C_CQ = (0, 256)
C_CKV = (256, 384)
C_KR = (384, 640)
C_FQ = (640, 1152)
C_FK = (1152, 1664)
C_FV = (1664, 2176)
W1_COLS = 2176
FL_ROWS = 16

VMEM_LIMIT = 56 * 1024 * 1024


def _cparams(sem):
    return pltpu.CompilerParams(dimension_semantics=sem, vmem_limit_bytes=VMEM_LIMIT)


def _rms(x, g):
    return x * lax.rsqrt(jnp.mean(x * x, axis=-1, keepdims=True) + RMS_EPS) * g


def _dot(a, b):
    return jnp.dot(a, b, preferred_element_type=F32)


def _dot_nt(a, b):
    return lax.dot_general(a, b, (((1,), (1,)), ((), ())), preferred_element_type=F32)


def _ada_kernel(c_ref, w_ref, b_ref, o_ref):
    c = c_ref[...]
    ca = (c * jax.nn.sigmoid(c)).astype(BF16)
    o_ref[...] = _dot(ca, w_ref[...].astype(BF16)) + b_ref[...]


def _ada(c, w, b):
    bsz, d = c.shape
    n = w.shape[1]
    bn = 1024
    return pl.pallas_call(
        _ada_kernel,
        grid=(n // bn,),
        in_specs=[
            pl.BlockSpec((bsz, d), lambda j: (0, 0)),
            pl.BlockSpec((d, bn), lambda j: (0, j)),
            pl.BlockSpec((1, bn), lambda j: (0, j)),
        ],
        out_specs=pl.BlockSpec((bsz, bn), lambda j: (0, j)),
        out_shape=jax.ShapeDtypeStruct((bsz, n), F32),
        compiler_params=_cparams(("arbitrary",)),
        name="ada",
    )(c, w, b.reshape(1, n))


def _pre_attn_kernel(x_ref, sc_ref, sh_ref, gpre_ref, w1_ref, wfl_ref, bf_ref, gq_ref, wqa_ref,
                     wqb_ref, gkv_ref, wuk_ref, wuv_ref, cp_ref, sp_ref,
                     q_ref, k_ref, v_ref, fq_ref, fk_ref, fv_ref, fc_ref, carry_ref):
    si = pl.program_id(1)
    ts = x_ref.shape[1]
    x = x_ref[0]
    h = _rms(x, gpre_ref[...]) * (1.0 + sc_ref[0]) + sh_ref[0]
    hb = h.astype(BF16)

    def proj(c):
        return _dot(hb, w1_ref[:, c[0]:c[1]])

    cp = cp_ref[0]
    sp = sp_ref[0]

    cqn = _rms(proj(C_CQ), gq_ref[...]).astype(BF16)
    qa = _dot(cqn, wqa_ref[...])
    qb = _dot(cqn, wqb_ref[...])
    for hd in range(MLA_HEADS):
        sl = slice(hd * LANES, (hd + 1) * LANES)
        q_ref[0, :, sl] = ((qa[:, sl] * cp + qb[:, sl] * sp) * MLA_SCALE).astype(BF16)

    kr = proj(C_KR)
    k_rope = kr[:, :LANES] * cp + kr[:, LANES:] * sp
    ckvn = _rms(proj(C_CKV), gkv_ref[...]).astype(BF16)
    kn = _dot(ckvn, wuk_ref[...])
    for hd in range(MLA_HEADS):
        sl = slice(hd * LANES, (hd + 1) * LANES)
        k_ref[0, :, sl] = (kn[:, sl] + k_rope).astype(BF16)
    v_ref[0] = _dot(ckvn, wuv_ref[...]).astype(BF16)

    fq_ref[0] = (proj(C_FQ) * FOX_SCALE).astype(BF16)
    fk_ref[0] = proj(C_FK).astype(BF16)
    fv_ref[0] = proj(C_FV).astype(BF16)

    fl = _dot_nt(wfl_ref[...], hb) + bf_ref[...]
    lf = jnp.minimum(fl, 0.0) - jnp.log1p(jnp.exp(-jnp.abs(fl)))
    r = lax.broadcasted_iota(jnp.int32, (ts, ts), 0)
    c = lax.broadcasted_iota(jnp.int32, (ts, ts), 1)
    tri = (r <= c).astype(BF16)
    p0 = lf.astype(BF16)
    r1 = lf - p0.astype(F32)
    p1 = r1.astype(BF16)
    p2 = (r1 - p1.astype(F32)).astype(BF16)
    cs = _dot(p0, tri) + _dot(p1, tri) + _dot(p2, tri)

    @pl.when(si == 0)
    def _():
        carry_ref[...] = jnp.zeros_like(carry_ref)

    cs = cs + carry_ref[:, 0:1]
    fc_ref[0] = cs
    carry_ref[...] = jnp.broadcast_to(cs[:, ts - 1:ts], carry_ref.shape)


def _pre_attn(x, sc, sh, g_pre, w1, wfl, bfg, g_q, wqa, wqb, g_kv, wuk, wuv, cp, sp, ts):
    bsz, s, d = x.shape
    grid = (bsz, s // ts)
    tok = lambda width: pl.BlockSpec((1, ts, width), lambda b, i: (b, i, 0))
    per_b = pl.BlockSpec((1, 1, d), lambda b, i: (b, 0, 0))
    full = lambda a: pl.BlockSpec(a.shape, lambda b, i: (0,) * a.ndim)
    outs = [
        jax.ShapeDtypeStruct((bsz, s, MLA_HEADS * LANES), BF16),
        jax.ShapeDtypeStruct((bsz, s, MLA_HEADS * LANES), BF16),
        jax.ShapeDtypeStruct((bsz, s, MLA_HEADS * V_DIM), BF16),
        jax.ShapeDtypeStruct((bsz, s, FOX_WIDTH), BF16),
        jax.ShapeDtypeStruct((bsz, s, FOX_WIDTH), BF16),
        jax.ShapeDtypeStruct((bsz, s, FOX_WIDTH), BF16),
        jax.ShapeDtypeStruct((bsz, FL_ROWS, s), F32),
    ]
    out_specs = [tok(MLA_HEADS * LANES), tok(MLA_HEADS * LANES), tok(MLA_HEADS * V_DIM),
                 tok(FOX_WIDTH), tok(FOX_WIDTH), tok(FOX_WIDTH),
                 pl.BlockSpec((1, FL_ROWS, ts), lambda b, i: (b, 0, i))]
    return pl.pallas_call(
        _pre_attn_kernel,
        grid=grid,
        in_specs=[tok(d), per_b, per_b, full(g_pre), full(w1), full(wfl), full(bfg), full(g_q),
                  full(wqa), full(wqb), full(g_kv), full(wuk), full(wuv), tok(LANES), tok(LANES)],
        out_specs=out_specs,
        out_shape=outs,
        scratch_shapes=[pltpu.VMEM((FL_ROWS, LANES), F32)],
        compiler_params=_cparams(("arbitrary", "arbitrary")),
        name="pre_attn",
    )(x, sc, sh, g_pre, w1, wfl, bfg, g_q, wqa, wqb, g_kv, wuk, wuv, cp, sp)


def _attn_kernel(*refs, tq, head_lanes, has_bias):
    if has_bias:
        q_ref, k_ref, v_ref, f_ref, o_ref, acc_ref, m_ref, l_ref = refs
    else:
        q_ref, k_ref, v_ref, o_ref, acc_ref, m_ref, l_ref = refs
        f_ref = None
    qi = pl.program_id(2)
    lane = lax.broadcasted_iota(jnp.int32, (tq, LANES), 1)
    if head_lanes == LANES:
        qs = [q_ref[0, :, 0:LANES], q_ref[0, :, LANES:2 * LANES]]
    else:
        q2 = q_ref[0]
        zero = jnp.zeros_like(q2)
        qs = [jnp.where(lane < head_lanes, q2, zero), jnp.where(lane >= head_lanes, q2, zero)]

    acc_ref[...] = jnp.zeros_like(acc_ref)
    m_ref[...] = jnp.full(m_ref.shape, NEG_INF, F32)
    l_ref[...] = jnp.zeros_like(l_ref)

    def step(off, width, masked):
        v2 = v_ref[0, pl.ds(off, width), :]
        for hd in range(2):
            if head_lanes == LANES:
                kk = k_ref[0, pl.ds(off, width), hd * LANES:(hd + 1) * LANES]
            else:
                kk = k_ref[0, pl.ds(off, width), :]
            s = _dot_nt(qs[hd], kk)
            if has_bias:
                s = s - f_ref[0, 0, hd:hd + 1, pl.ds(off, width)]
            if masked:
                rr = lax.broadcasted_iota(jnp.int32, (tq, width), 0)
                cc = lax.broadcasted_iota(jnp.int32, (tq, width), 1)
                s = jnp.where(cc <= rr, s, NEG_INF)
            m = m_ref[hd]
            m_new = jnp.maximum(m, jnp.max(s, axis=-1, keepdims=True))
            alpha = jnp.exp(m - m_new)
            p = jnp.exp(s - jnp.tile(m_new, (1, width // LANES)))
            l_ref[hd] = alpha * l_ref[hd] + jnp.sum(p, axis=-1, keepdims=True)
            m_ref[hd] = m_new
            acc_ref[hd] = alpha * acc_ref[hd] + _dot(p.astype(BF16), v2)

    def wide(j, _):
        step(pl.multiple_of(j * (2 * tq), 2 * tq), 2 * tq, False)
        return 0

    lax.fori_loop(0, qi // 2, wide, 0)

    @pl.when(qi % 2 == 1)
    def _():
        step(pl.multiple_of((qi - 1) * tq, tq), tq, False)

    step(pl.multiple_of(qi * tq, tq), tq, True)
    o0 = acc_ref[0] / l_ref[0]
    o1 = acc_ref[1] / l_ref[1]
    o_ref[0] = jnp.where(lane < V_DIM, o0, o1).astype(o_ref.dtype)


def _attention(q, k, v, fcum, tq, head_lanes):
    bsz, s, _ = q.shape
    n_pairs = v.shape[2] // LANES
    qk_w = 2 * head_lanes
    has_bias = fcum is not None
    in_specs = [
        pl.BlockSpec((1, tq, qk_w), lambda b, p, i: (b, i, p)),
        pl.BlockSpec((1, s, qk_w), lambda b, p, i: (b, 0, p)),
        pl.BlockSpec((1, s, LANES), lambda b, p, i: (b, 0, p)),
    ]
    args = [q, k, v]
    if has_bias:
        in_specs.append(pl.BlockSpec((1, 1, 2, s), lambda b, p, i: (b, p, 0, 0)))
        args.append(fcum)
    return pl.pallas_call(
        functools.partial(_attn_kernel, tq=tq, head_lanes=head_lanes, has_bias=has_bias),
        grid=(bsz, n_pairs, s // tq),
        in_specs=in_specs,
        out_specs=pl.BlockSpec((1, tq, LANES), lambda b, p, i: (b, i, p)),
        out_shape=jax.ShapeDtypeStruct((bsz, s, n_pairs * LANES), BF16),
        scratch_shapes=[pltpu.VMEM((2, tq, LANES), F32), pltpu.VMEM((2, tq, LANES), F32),
                        pltpu.VMEM((2, tq, LANES), F32)],
        compiler_params=_cparams(("arbitrary", "arbitrary", "arbitrary")),
        name="attn_fox" if has_bias else "attn_mla",
    )(*args)


def _pack_bf16_pairs(xb):
    w = xb.shape[1] // 2
    lo = lax.bitcast_convert_type(xb[:, :w].astype(F32), jnp.uint32)
    hi = lax.bitcast_convert_type(xb[:, w:].astype(F32), jnp.uint32)
    return (hi & jnp.uint32(0xFFFF0000)) | lax.shift_right_logical(lo, jnp.uint32(16))


def _unpack_bf16_pairs(xw):
    lo = lax.bitcast_convert_type(lax.shift_left(xw, jnp.uint32(16)), F32)
    hi = lax.bitcast_convert_type(xw & jnp.uint32(0xFFFF0000), F32)
    return jnp.concatenate([lo, hi], axis=1).astype(BF16)


def _post_attn_kernel(om_ref, of_ref, x_ref, wo_ref, gpost_ref, gate_ref, gmoe_ref, sc_ref, sh_ref,
                      wr_ref, br_ref, x1_ref, h2_ref, idx_ref, gts_ref):
    half = om_ref.shape[2]
    o = _dot(om_ref[0], wo_ref[0:half, :]) + _dot(of_ref[0], wo_ref[half:, :])
    x1 = x_ref[0] + gate_ref[0] * _rms(o, gpost_ref[...])
    x1_ref[0] = x1
    h2 = _rms(x1, gmoe_ref[...]) * (1.0 + sc_ref[0]) + sh_ref[0]
    hb = h2.astype(BF16)
    h2_ref[0] = _pack_bf16_pairs(hb)
    logits = _dot_nt(wr_ref[...], hb) + br_ref[...]
    n_e = logits.shape[0]
    eid = lax.broadcasted_iota(jnp.int32, logits.shape, 0)
    vals, idxs = [], []
    for _ in range(TOP_K):
        m = jnp.max(logits, axis=0, keepdims=True)
        ix = jnp.min(jnp.where(logits == m, eid, n_e), axis=0, keepdims=True)
        vals.append(m)
        idxs.append(ix)
        logits = jnp.where(eid == ix, -jnp.inf, logits)
    es = [jnp.exp(vv - vals[0]) for vv in vals]
    den = es[0] + es[1] + es[2] + es[3]
    for kk in range(TOP_K):
        idx_ref[0, kk:kk + 1, :] = idxs[kk]
        gts_ref[0, kk:kk + 1, :] = es[kk] / den


def _post_attn(o_mla, o_fox, x, wo, g_post, gate, g_moe, sc, sh, wr_t, br, ts):
    bsz, s, d = x.shape
    half = o_mla.shape[2]
    tok = lambda width: pl.BlockSpec((1, ts, width), lambda b, i: (b, i, 0))
    per_b = pl.BlockSpec((1, 1, d), lambda b, i: (b, 0, 0))
    full = lambda a: pl.BlockSpec(a.shape, lambda b, i: (0,) * a.ndim)
    k_spec = pl.BlockSpec((1, TOP_K, ts), lambda b, i: (b, 0, i))
    return pl.pallas_call(
        _post_attn_kernel,
        grid=(bsz, s // ts),
        in_specs=[tok(half), tok(half), tok(d), full(wo), full(g_post), per_b, full(g_moe), per_b,
                  per_b, full(wr_t), full(br)],
        out_specs=[tok(d), tok(d // 2), k_spec, k_spec],
        out_shape=[jax.ShapeDtypeStruct((bsz, s, d), F32),
                   jax.ShapeDtypeStruct((bsz, s, d // 2), jnp.uint32),
                   jax.ShapeDtypeStruct((bsz, TOP_K, s), jnp.int32),
                   jax.ShapeDtypeStruct((bsz, TOP_K, s), F32)],
        compiler_params=_cparams(("arbitrary", "arbitrary")),
        name="post_attn",
    )(o_mla, o_fox, x, wo, g_post, gate, g_moe, sc, sh, wr_t, br)


def _route_kernel(idx_ref, tri_ref, rank_ref, cnt_ref, run_ref):
    first = jnp.logical_and(pl.program_id(0) == 0, pl.program_id(1) == 0)

    @pl.when(first)
    def _():
        run_ref[...] = jnp.zeros_like(run_ref)

    tr = idx_ref.shape[2]
    eid = lax.broadcasted_iota(jnp.int32, (N_EXPERTS, tr), 0)
    hot = [eid == idx_ref[0, kk:kk + 1, :] for kk in range(TOP_K)]
    multi = hot[0] | hot[1] | hot[2] | hot[3]
    mf = jnp.where(multi, 1.0, 0.0)
    incl = _dot(mf.astype(BF16), tri_ref[...]) + run_ref[:, 0:1]
    excl = incl - mf
    for kk in range(TOP_K):
        rank_ref[0, kk:kk + 1, :] = jnp.sum(jnp.where(hot[kk], excl, 0.0), axis=0,
                                            keepdims=True).astype(jnp.int32)
    run_ref[...] = jnp.broadcast_to(incl[:, tr - 1:tr], run_ref.shape)
    cnt_ref[...] = run_ref[...].astype(jnp.int32)


def _route(idx_kt, tr):
    bsz, _, s = idx_kt.shape
    tri = (jnp.arange(tr)[:, None] <= jnp.arange(tr)[None, :]).astype(BF16)
    k_spec = pl.BlockSpec((1, TOP_K, tr), lambda b, i: (b, 0, i))
    return pl.pallas_call(
        _route_kernel,
        grid=(bsz, s // tr),
        in_specs=[k_spec, pl.BlockSpec((tr, tr), lambda b, i: (0, 0))],
        out_specs=[k_spec, pl.BlockSpec((N_EXPERTS, LANES), lambda b, i: (0, 0))],
        out_shape=[jax.ShapeDtypeStruct((bsz, TOP_K, s), jnp.int32),
                   jax.ShapeDtypeStruct((N_EXPERTS, LANES), jnp.int32)],
        scratch_shapes=[pltpu.VMEM((N_EXPERTS, LANES), F32)],
        compiler_params=_cparams(("arbitrary", "arbitrary")),
        name="route",
    )(idx_kt, tri)


def _expert_kernel(be_ref, first_ref, nused_ref, gidx_ref, sidx_ref, h_hbm, wgu_ref, bgu_ref, wd_ref,
                   bd_ref, y_hbm, xbuf0, xbuf1, ybuf0, ybuf1, wgu_bf, wd_bf, sem_g, sem_s):
    i = pl.program_id(0)
    n_used = nused_ref[0]
    bm = xbuf0.shape[0]
    ff = wd_ref.shape[1]
    xbufs = (xbuf0, xbuf1)
    ybufs = (ybuf0, ybuf1)

    def gather_start(slot, r):
        pltpu.make_async_copy(h_hbm.at[pl.ds(gidx_ref[0, 0, r], 1), :],
                              xbufs[slot].at[pl.ds(r, 1), :], sem_g.at[slot]).start()

    def scatter_start(slot, r):
        pltpu.make_async_copy(ybufs[slot].at[pl.ds(r, 1), :],
                              y_hbm.at[pl.ds(sidx_ref[0, 0, r], 1), :], sem_s.at[slot]).start()

    def gather_wait(slot):
        pltpu.make_async_copy(h_hbm.at[pl.ds(0, bm), :], xbufs[slot], sem_g.at[slot]).wait()

    def scatter_wait(slot):
        pltpu.make_async_copy(ybufs[slot], y_hbm.at[pl.ds(0, bm), :], sem_s.at[slot]).wait()

    def rolled(start_fn, slot):
        def body(r, _):
            start_fn(slot, r)
            return 0
        lax.fori_loop(0, bm, body, 0, unroll=8)

    @pl.when(jnp.logical_and(first_ref[i] == 1, i <= n_used))
    def _():
        wgu_bf[...] = wgu_ref[0].astype(BF16)
        wd_bf[...] = wd_ref[0].astype(BF16)

    def step(slot_in):
        slot_cur = 1 - slot_in

        if slot_in == 0:
            @pl.when(i == 0)
            def _():
                ybuf0[...] = jnp.zeros_like(ybuf0)
                ybuf1[...] = jnp.zeros_like(ybuf1)
                rolled(gather_start, 0)

        @pl.when(jnp.logical_and(i >= 1, i <= n_used + 1))
        def _():
            gather_wait(slot_cur)

        @pl.when(jnp.logical_and(i >= 2, i <= n_used + 1))
        def _():
            scatter_wait(slot_cur)

        @pl.when(jnp.logical_and(i >= 1, i <= n_used))
        def _():
            xb = _unpack_bf16_pairs(xbufs[slot_cur][...])
            for r in range(bm):
                scatter_start(slot_in, r)
            for r in range(bm):
                gather_start(slot_in, r)
            gu = _dot(xb, wgu_bf[...]) + bgu_ref[0]
            g = jnp.minimum(gu[:, :ff], SWIGLU_LIMIT)
            u = jnp.clip(gu[:, ff:], -SWIGLU_LIMIT, SWIGLU_LIMIT)
            glu = g * jax.nn.sigmoid(SWIGLU_ALPHA * g)
            act = ((u + 1.0) * glu).astype(BF16)
            ybufs[slot_cur][...] = _dot(act, wd_bf[...]) + bd_ref[0]

        @pl.when(i == n_used + 1)
        def _():
            rolled(scatter_start, slot_in)
            scatter_wait(slot_in)

    @pl.when(i % 2 == 0)
    def _():
        step(0)

    @pl.when(i % 2 == 1)
    def _():
        step(1)


def _experts(be_step, first_step, n_used, gidx, sidx, h2p, w_gu, b_gu, w_d, b_d, n_out_rows, bm):
    e, d, f2 = w_gu.shape
    ff = w_d.shape[1]
    n_steps = gidx.shape[0]
    smem_blk = pl.BlockSpec((1, 1, bm), lambda i, be, fi, nu: (i, 0, 0), memory_space=pltpu.SMEM)
    return pl.pallas_call(
        _expert_kernel,
        grid_spec=pltpu.PrefetchScalarGridSpec(
            num_scalar_prefetch=3,
            grid=(n_steps,),
            in_specs=[
                smem_blk,
                smem_blk,
                pl.BlockSpec(memory_space=pl.ANY),
                pl.BlockSpec((1, d, f2), lambda i, be, fi, nu: (be[i], 0, 0)),
                pl.BlockSpec((1, 1, f2), lambda i, be, fi, nu: (be[i], 0, 0)),
                pl.BlockSpec((1, ff, d), lambda i, be, fi, nu: (be[i], 0, 0)),
                pl.BlockSpec((1, 1, d), lambda i, be, fi, nu: (be[i], 0, 0)),
            ],
            out_specs=pl.BlockSpec(memory_space=pl.ANY),
            scratch_shapes=[pltpu.VMEM((bm, d // 2), jnp.uint32), pltpu.VMEM((bm, d // 2), jnp.uint32),
                            pltpu.VMEM((bm, d), F32), pltpu.VMEM((bm, d), F32),
                            pltpu.VMEM((d, f2), BF16), pltpu.VMEM((ff, d), BF16),
                            pltpu.SemaphoreType.DMA((2,)), pltpu.SemaphoreType.DMA((2,))],
        ),
        out_shape=jax.ShapeDtypeStruct((n_out_rows, d), F32),
        compiler_params=_cparams(("arbitrary",)),
        name="experts",
    )(be_step, first_step, n_used, gidx, sidx, h2p, w_gu, b_gu.reshape(e, 1, f2), w_d,
      b_d.reshape(e, 1, d))


def _combine_kernel(y0_ref, y1_ref, y2_ref, y3_ref, gts_ref, x1_ref, gpost_ref, gate_ref, o_ref):
    gts = gts_ref[...]
    acc = gts[:, 0:1] * y0_ref[...]
    for kk, y_ref in enumerate((y1_ref, y2_ref, y3_ref), start=1):
        acc = acc + gts[:, kk:kk + 1] * y_ref[...]
    o_ref[0] = x1_ref[0] + gate_ref[0] * _rms(acc, gpost_ref[...])


def _combine(y4, gts, x1, g_post, gate, tt):
    bsz, s, d = x1.shape
    n_t = s // tt
    n_tiles = bsz * n_t
    y_spec = lambda kk: pl.BlockSpec((tt, d), lambda b, i: (kk * n_tiles + b * n_t + i, 0))
    return pl.pallas_call(
        _combine_kernel,
        grid=(bsz, n_t),
        in_specs=[
            y_spec(0), y_spec(1), y_spec(2), y_spec(3),
            pl.BlockSpec((tt, TOP_K), lambda b, i: (b * n_t + i, 0)),
            pl.BlockSpec((1, tt, d), lambda b, i: (b, i, 0)),
            pl.BlockSpec(g_post.shape, lambda b, i: (0, 0)),
            pl.BlockSpec((1, 1, d), lambda b, i: (b, 0, 0)),
        ],
        out_specs=pl.BlockSpec((1, tt, d), lambda b, i: (b, i, 0)),
        out_shape=jax.ShapeDtypeStruct((bsz, s, d), F32),
        compiler_params=_cparams(("arbitrary", "arbitrary")),
        name="combine",
    )(y4, y4, y4, y4, gts, x1, g_post, gate)


def _rope_patterns(positions):
    inv_freq = ROPE_THETA ** (-jnp.arange(0, ROPE_DIM, 2, dtype=F32) / ROPE_DIM)
    ang = positions.astype(F32)[..., None] * inv_freq
    cos, sin = jnp.cos(ang), jnp.sin(ang)
    ones = jnp.ones(positions.shape + (NOPE_DIM,), F32)
    zeros = jnp.zeros(positions.shape + (LANES - NOPE_DIM - ROPE_DIM,), F32)
    cp = jnp.concatenate([ones, cos, cos, zeros], axis=-1)
    sp = jnp.concatenate([0.0 * ones, -sin, sin, zeros], axis=-1)
    return cp, sp


def _prep_mixer_weights(w_in, w_uq, w_ukv, b_forget):
    d = w_in.shape[0]
    o_kr = Q_RANK + KV_RANK
    o_f = o_kr + ROPE_DIM
    hr = ROPE_DIM // 2
    z = lambda n: jnp.zeros((d, n), w_in.dtype)
    kr = w_in[:, o_kr:o_f]
    kr_sw = jnp.concatenate([kr[:, hr:], kr[:, :hr]], axis=1)
    pad = LANES - NOPE_DIM - ROPE_DIM
    w1 = jnp.concatenate([
        w_in[:, :o_kr],
        z(NOPE_DIM), kr, z(pad),
        z(NOPE_DIM), kr_sw, z(pad),
        w_in[:, o_f:o_f + 3 * FOX_WIDTH],
    ], axis=1).astype(BF16)
    wfl = jnp.zeros((FL_ROWS, d), w_in.dtype).at[:FOX_HEADS].set(
        w_in[:, o_f + 3 * FOX_WIDTH:].T).astype(BF16)
    bfg = jnp.zeros((FL_ROWS, 1), F32).at[:FOX_HEADS, 0].set(b_forget)

    wq = w_uq.reshape(Q_RANK, MLA_HEADS, NOPE_DIM + ROPE_DIM)
    nope, rope = wq[..., :NOPE_DIM], wq[..., NOPE_DIM:]
    zq = lambda n: jnp.zeros((Q_RANK, MLA_HEADS, n), w_uq.dtype)
    wqa = jnp.concatenate([nope, rope, zq(pad)], axis=-1).reshape(Q_RANK, -1).astype(BF16)
    wqb = jnp.concatenate([zq(NOPE_DIM), rope[..., hr:], rope[..., :hr], zq(pad)],
                          axis=-1).reshape(Q_RANK, -1).astype(BF16)
    wkv = w_ukv.reshape(KV_RANK, MLA_HEADS, NOPE_DIM + V_DIM)
    wuk = jnp.concatenate([wkv[..., :NOPE_DIM],
                           jnp.zeros((KV_RANK, MLA_HEADS, LANES - NOPE_DIM), w_ukv.dtype)],
                          axis=-1).reshape(KV_RANK, -1).astype(BF16)
    wuv = wkv[..., NOPE_DIM:].reshape(KV_RANK, -1).astype(BF16)
    return w1, wfl, bfg, wqa, wqb, wuk, wuv


def _slot_tables(idx_kt, rank_kt, counts, bm):
    bsz, _, s = idx_kt.shape
    t = bsz * s
    n_blocks = (t * TOP_K) // bm + N_EXPERTS
    n_real = t * TOP_K
    i32 = jnp.int32
    eids = jnp.arange(N_EXPERTS, dtype=i32)

    pcounts = ((counts + bm - 1) // bm) * bm
    incl_mat = (eids[:, None] <= eids[None, :]).astype(i32)
    pends = pcounts @ incl_mat
    pstarts = pends - pcounts
    cstarts = counts @ incl_mat - counts
    total = pends[-1]
    n_used = (total // bm).astype(i32).reshape(1)

    onehot = idx_kt[..., None] == eids
    dest = jnp.sum(jnp.where(onehot, pstarts, 0), axis=-1) + rank_kt
    tok = jnp.arange(t, dtype=i32).reshape(bsz, 1, s)
    kk = jnp.arange(TOP_K, dtype=i32).reshape(1, TOP_K, 1)
    yrow_real = jnp.broadcast_to(kk * t + tok, dest.shape)
    _, rows_sorted = lax.sort((dest.reshape(-1), yrow_real.reshape(-1)), num_keys=1)

    blk0 = jnp.arange(n_blocks, dtype=i32)
    be_blk = jnp.minimum(jnp.sum((pends[None, :] <= (blk0 * bm)[:, None]).astype(i32), axis=1),
                         N_EXPERTS - 1)
    hot_b = be_blk[:, None] == eids[None, :]
    pick = lambda v: jnp.sum(jnp.where(hot_b, v[None, :], 0), axis=1)[:, None]
    local = (blk0 * bm)[:, None] + jnp.arange(bm, dtype=i32)[None, :] - pick(pstarts)
    valid = jnp.logical_and(local < pick(counts), (blk0 < n_used[0])[:, None])
    j = jnp.clip(pick(cstarts) + local, 0, n_real - 1)
    rows = jnp.take(rows_sorted, j.reshape(-1), axis=0).reshape(n_blocks, bm)
    spare = TOP_K * t + jnp.arange(bm, dtype=i32)[None, :]
    yrow = jnp.where(valid, rows, spare)
    src_tok = jnp.where(valid, rows % t, 0)

    zeros2 = jnp.zeros((2, bm), i32)
    spare2 = jnp.broadcast_to(spare, (2, bm))
    gidx = jnp.concatenate([src_tok, zeros2]).reshape(n_blocks + 2, 1, bm)
    sidx = jnp.concatenate([spare2, yrow]).reshape(n_blocks + 2, 1, bm)

    blk = jnp.arange(n_blocks + 2, dtype=i32) - 1
    be = jnp.sum((pends[None, :] <= (jnp.maximum(blk, 0) * bm)[:, None]).astype(i32), axis=1)
    be_step = jnp.minimum(be, N_EXPERTS - 1)
    first_step = jnp.concatenate([jnp.ones((2,), i32),
                                  (be_step[2:] != be_step[1:-1]).astype(i32)])
    return gidx, sidx, be_step, first_step, n_used


def _layer(x, ada, cp, sp, g_attn_pre, g_attn_post, w_in, g_q_norm, w_uq, g_kv_norm, w_ukv,
           b_forget, w_out, g_moe_pre, g_moe_post, w_router, b_router, w_gate_up, b_gate_up,
           w_down, b_down):
    bsz, s, d = x.shape
    t = bsz * s
    ts = min(512, s)
    tq = min(512, s)
    tt = min(512, s)
    tr = min(1024, s)
    bm = 512
    row = lambda v: v.reshape(1, -1)
    sh_a, sc_a, gt_a, sh_m, sc_m, gt_m = [v.reshape(bsz, 1, d) for v in jnp.split(ada, 6, axis=-1)]

    w1, wfl, bfg, wqa, wqb, wuk, wuv = _prep_mixer_weights(w_in, w_uq, w_ukv, b_forget)
    q, k, v, fq, fk, fv, fcum = _pre_attn(x, sc_a, sh_a, row(g_attn_pre), w1, wfl, bfg,
                                          row(g_q_norm), wqa, wqb, row(g_kv_norm), wuk, wuv,
                                          cp, sp, ts)
    o_mla = _attention(q, k, v, None, tq, LANES)
    fpairs = fcum[:, :FOX_HEADS].reshape(bsz, FOX_HEADS // 2, 2, s)
    o_fox = _attention(fq, fk, fv, fpairs, tq, FOX_DIM)

    x1, h2, idx_kt, gts_kt = _post_attn(o_mla, o_fox, x, w_out.astype(BF16), row(g_attn_post), gt_a,
                                        row(g_moe_pre), sc_m, sh_m, w_router.T.astype(BF16),
                                        b_router.reshape(-1, 1), ts)

    rank_kt, cnt = _route(idx_kt, tr)
    gidx, sidx, be_step, first_step, n_used = _slot_tables(idx_kt, rank_kt, cnt[:, 0], bm)
    y4 = _experts(be_step, first_step, n_used, gidx, sidx, h2.reshape(t, d // 2), w_gate_up, b_gate_up,
                  w_down, b_down, TOP_K * t + bm, bm)
    gts = gts_kt.transpose(0, 2, 1).reshape(t, TOP_K)
    return _combine(y4, gts, x1, row(g_moe_post), gt_m, tt)


def kernel(x, c, positions, w_ada, b_ada, g_attn_pre, g_attn_post, w_in, g_q_norm, w_uq, g_kv_norm,
           w_ukv, b_forget, w_out, g_moe_pre, g_moe_post, w_router, b_router, w_gate_up, b_gate_up,
           w_down, b_down):
    cp, sp = _rope_patterns(positions)
    for layer in range(w_ada.shape[0]):
        ada = _ada(c, w_ada[layer], b_ada[layer])
        x = _layer(x, ada, cp, sp, g_attn_pre[layer], g_attn_post[layer], w_in[layer],
                   g_q_norm[layer], w_uq[layer], g_kv_norm[layer], w_ukv[layer], b_forget[layer],
                   w_out[layer], g_moe_pre[layer], g_moe_post[layer], w_router[layer],
                   b_router[layer], w_gate_up[layer], b_gate_up[layer], w_down[layer],
                   b_down[layer])
    return x
```

```python
import functools
import math

import jax
import jax.numpy as jnp
from jax import lax
from jax.experimental import pallas as pl
from jax.experimental.pallas import tpu as pltpu

F32 = jnp.float32
BF16 = jnp.bfloat16

MLA_HEADS = 8
NOPE_DIM = 64
ROPE_DIM = 32
V_DIM = 64
Q_RANK = 256
KV_RANK = 128
FOX_HEADS = 8
FOX_DIM = 64
FOX_WIDTH = FOX_HEADS * FOX_DIM
ROPE_THETA = 10000.0
N_EXPERTS = 32
TOP_K = 4
SWIGLU_ALPHA = 1.702
SWIGLU_LIMIT = 7.0
RMS_EPS = 1e-6
NEG_INF = -1e30
LOG2E = math.log2(math.e)
MLA_SCALE = LOG2E / math.sqrt(NOPE_DIM + ROPE_DIM)
FOX_SCALE = LOG2E / math.sqrt(FOX_DIM)

LANES = 128
C_CQ = (0, 256)
C_CKV = (256, 384)
C_KR = (384, 640)
C_FQ = (640, 1152)
C_FK = (1152, 1664)
C_FV = (1664, 2176)
W1_COLS = 2176
FL_ROWS = 16

VMEM_LIMIT = 56 * 1024 * 1024


def _cparams(sem):
    return pltpu.CompilerParams(dimension_semantics=sem, vmem_limit_bytes=VMEM_LIMIT)


def _rms(x, g):
    return x * lax.rsqrt(jnp.mean(x * x, axis=-1, keepdims=True) + RMS_EPS) * g


def _dot(a, b):
    return jnp.dot(a, b, preferred_element_type=F32)


def _dot_nt(a, b):
    return lax.dot_general(a, b, (((1,), (1,)), ((), ())), preferred_element_type=F32)


def _ada_kernel(c_ref, w_ref, b_ref, o_ref):
    c = c_ref[...]
    ca = (c * jax.nn.sigmoid(c)).astype(BF16)
    o_ref[...] = _dot(ca, w_ref[...].astype(BF16)) + b_ref[...]


def _ada(c, w, b):
    bsz, d = c.shape
    n = w.shape[1]
    bn = 1024
    return pl.pallas_call(
        _ada_kernel,
        grid=(n // bn,),
        in_specs=[
            pl.BlockSpec((bsz, d), lambda j: (0, 0)),
            pl.BlockSpec((d, bn), lambda j: (0, j)),
            pl.BlockSpec((1, bn), lambda j: (0, j)),
        ],
        out_specs=pl.BlockSpec((bsz, bn), lambda j: (0, j)),
        out_shape=jax.ShapeDtypeStruct((bsz, n), F32),
        compiler_params=_cparams(("arbitrary",)),
        name="ada",
    )(c, w, b.reshape(1, n))


def _pre_attn_kernel(x_ref, sc_ref, sh_ref, gpre_ref, w1_ref, wfl_ref, bf_ref, gq_ref, wqa_ref,
                     wqb_ref, gkv_ref, wuk_ref, wuv_ref, cp_ref, sp_ref,
                     q_ref, k_ref, v_ref, fq_ref, fk_ref, fv_ref, fc_ref, carry_ref):
    si = pl.program_id(1)
    ts = x_ref.shape[1]
    x = x_ref[0]
    h = _rms(x, gpre_ref[...]) * (1.0 + sc_ref[0]) + sh_ref[0]
    hb = h.astype(BF16)

    def proj(c):
        return _dot(hb, w1_ref[:, c[0]:c[1]])

    cp = cp_ref[0]
    sp = sp_ref[0]

    cqn = _rms(proj(C_CQ), gq_ref[...]).astype(BF16)
    qa = _dot(cqn, wqa_ref[...])
    qb = _dot(cqn, wqb_ref[...])
    for hd in range(MLA_HEADS):
        sl = slice(hd * LANES, (hd + 1) * LANES)
        q_ref[0, :, sl] = ((qa[:, sl] * cp + qb[:, sl] * sp) * MLA_SCALE).astype(BF16)

    kr = proj(C_KR)
    k_rope = kr[:, :LANES] * cp + kr[:, LANES:] * sp
    ckvn = _rms(proj(C_CKV), gkv_ref[...]).astype(BF16)
    kn = _dot(ckvn, wuk_ref[...])
    for hd in range(MLA_HEADS):
        sl = slice(hd * LANES, (hd + 1) * LANES)
        k_ref[0, :, sl] = (kn[:, sl] + k_rope).astype(BF16)
    v_ref[0] = _dot(ckvn, wuv_ref[...]).astype(BF16)

    fq_ref[0] = (proj(C_FQ) * FOX_SCALE).astype(BF16)
    fk_ref[0] = proj(C_FK).astype(BF16)
    fv_ref[0] = proj(C_FV).astype(BF16)

    fl = _dot_nt(wfl_ref[...], hb) + bf_ref[...]
    lf = jnp.minimum(fl, 0.0) - jnp.log1p(jnp.exp(-jnp.abs(fl)))
    r = lax.broadcasted_iota(jnp.int32, (ts, ts), 0)
    c = lax.broadcasted_iota(jnp.int32, (ts, ts), 1)
    tri = (r <= c).astype(BF16)
    p0 = lf.astype(BF16)
    r1 = lf - p0.astype(F32)
    p1 = r1.astype(BF16)
    p2 = (r1 - p1.astype(F32)).astype(BF16)
    cs = _dot(p0, tri) + _dot(p1, tri) + _dot(p2, tri)

    @pl.when(si == 0)
    def _():
        carry_ref[...] = jnp.zeros_like(carry_ref)

    cs = cs + carry_ref[:, 0:1]
    fc_ref[0] = cs * LOG2E
    carry_ref[...] = jnp.broadcast_to(cs[:, ts - 1:ts], carry_ref.shape)


def _pre_attn(x, sc, sh, g_pre, w1, wfl, bfg, g_q, wqa, wqb, g_kv, wuk, wuv, cp, sp, ts):
    bsz, s, d = x.shape
    grid = (bsz, s // ts)
    tok = lambda width: pl.BlockSpec((1, ts, width), lambda b, i: (b, i, 0))
    per_b = pl.BlockSpec((1, 1, d), lambda b, i: (b, 0, 0))
    full = lambda a: pl.BlockSpec(a.shape, lambda b, i: (0,) * a.ndim)
    outs = [
        jax.ShapeDtypeStruct((bsz, s, MLA_HEADS * LANES), BF16),
        jax.ShapeDtypeStruct((bsz, s, MLA_HEADS * LANES), BF16),
        jax.ShapeDtypeStruct((bsz, s, MLA_HEADS * V_DIM), BF16),
        jax.ShapeDtypeStruct((bsz, s, FOX_WIDTH), BF16),
        jax.ShapeDtypeStruct((bsz, s, FOX_WIDTH), BF16),
        jax.ShapeDtypeStruct((bsz, s, FOX_WIDTH), BF16),
        jax.ShapeDtypeStruct((bsz, FL_ROWS, s), F32),
    ]
    out_specs = [tok(MLA_HEADS * LANES), tok(MLA_HEADS * LANES), tok(MLA_HEADS * V_DIM),
                 tok(FOX_WIDTH), tok(FOX_WIDTH), tok(FOX_WIDTH),
                 pl.BlockSpec((1, FL_ROWS, ts), lambda b, i: (b, 0, i))]
    return pl.pallas_call(
        _pre_attn_kernel,
        grid=grid,
        in_specs=[tok(d), per_b, per_b, full(g_pre), full(w1), full(wfl), full(bfg), full(g_q),
                  full(wqa), full(wqb), full(g_kv), full(wuk), full(wuv), tok(LANES), tok(LANES)],
        out_specs=out_specs,
        out_shape=outs,
        scratch_shapes=[pltpu.VMEM((FL_ROWS, LANES), F32)],
        compiler_params=_cparams(("arbitrary", "arbitrary")),
        name="pre_attn",
    )(x, sc, sh, g_pre, w1, wfl, bfg, g_q, wqa, wqb, g_kv, wuk, wuv, cp, sp)


def _attn_kernel(*refs, tq, head_lanes, has_bias):
    if has_bias:
        q_ref, k_ref, v_ref, f_ref, o_ref, acc_ref, m_ref, l_ref = refs
    else:
        q_ref, k_ref, v_ref, o_ref, acc_ref, m_ref, l_ref = refs
        f_ref = None
    qi = pl.program_id(2)
    lane = lax.broadcasted_iota(jnp.int32, (tq, LANES), 1)
    if head_lanes == LANES:
        qs = [q_ref[0, :, 0:LANES], q_ref[0, :, LANES:2 * LANES]]
    else:
        q2 = q_ref[0]
        zero = jnp.zeros_like(q2)
        qs = [jnp.where(lane < head_lanes, q2, zero), jnp.where(lane >= head_lanes, q2, zero)]

    acc_ref[...] = jnp.zeros_like(acc_ref)
    m_ref[...] = jnp.full(m_ref.shape, NEG_INF, F32)
    l_ref[...] = jnp.zeros_like(l_ref)

    def step(off, width, masked):
        v2 = v_ref[0, pl.ds(off, width), :]
        for hd in range(2):
            if head_lanes == LANES:
                kk = k_ref[0, pl.ds(off, width), hd * LANES:(hd + 1) * LANES]
            else:
                kk = k_ref[0, pl.ds(off, width), :]
            s = _dot_nt(qs[hd], kk)
            if has_bias:
                s = s - f_ref[0, 0, hd:hd + 1, pl.ds(off, width)]
            if masked:
                rr = lax.broadcasted_iota(jnp.int32, (tq, width), 0)
                cc = lax.broadcasted_iota(jnp.int32, (tq, width), 1)
                s = jnp.where(cc <= rr, s, NEG_INF)
            m = m_ref[hd]
            m_new = jnp.maximum(m, jnp.max(s, axis=-1, keepdims=True))
            alpha = jnp.exp2(m - m_new)
            p = jnp.exp2(s - jnp.tile(m_new, (1, width // LANES)))
            l_ref[hd] = alpha * l_ref[hd] + jnp.sum(p, axis=-1, keepdims=True)
            m_ref[hd] = m_new
            acc_ref[hd] = alpha * acc_ref[hd] + _dot(p.astype(BF16), v2)

    def wide(j, _):
        step(pl.multiple_of(j * (2 * tq), 2 * tq), 2 * tq, False)
        return 0

    lax.fori_loop(0, qi // 2, wide, 0)

    @pl.when(qi % 2 == 1)
    def _():
        step(pl.multiple_of((qi - 1) * tq, tq), tq, False)

    step(pl.multiple_of(qi * tq, tq), tq, True)
    o0 = acc_ref[0] / l_ref[0]
    o1 = acc_ref[1] / l_ref[1]
    o_ref[0] = jnp.where(lane < V_DIM, o0, o1).astype(o_ref.dtype)


def _attention(q, k, v, fcum, tq, head_lanes):
    bsz, s, _ = q.shape
    n_pairs = v.shape[2] // LANES
    qk_w = 2 * head_lanes
    has_bias = fcum is not None
    in_specs = [
        pl.BlockSpec((1, tq, qk_w), lambda b, p, i: (b, i, p)),
        pl.BlockSpec((1, s, qk_w), lambda b, p, i: (b, 0, p)),
        pl.BlockSpec((1, s, LANES), lambda b, p, i: (b, 0, p)),
    ]
    args = [q, k, v]
    if has_bias:
        in_specs.append(pl.BlockSpec((1, 1, 2, s), lambda b, p, i: (b, p, 0, 0)))
        args.append(fcum)
    return pl.pallas_call(
        functools.partial(_attn_kernel, tq=tq, head_lanes=head_lanes, has_bias=has_bias),
        grid=(bsz, n_pairs, s // tq),
        in_specs=in_specs,
        out_specs=pl.BlockSpec((1, tq, LANES), lambda b, p, i: (b, i, p)),
        out_shape=jax.ShapeDtypeStruct((bsz, s, n_pairs * LANES), BF16),
        scratch_shapes=[pltpu.VMEM((2, tq, LANES), F32), pltpu.VMEM((2, tq, LANES), F32),
                        pltpu.VMEM((2, tq, LANES), F32)],
        compiler_params=_cparams(("arbitrary", "arbitrary", "arbitrary")),
        name="attn_fox" if has_bias else "attn_mla",
    )(*args)


def _pack_bf16_pairs(xb):
    w = xb.shape[1] // 2
    lo = lax.bitcast_convert_type(xb[:, :w].astype(F32), jnp.uint32)
    hi = lax.bitcast_convert_type(xb[:, w:].astype(F32), jnp.uint32)
    return (hi & jnp.uint32(0xFFFF0000)) | lax.shift_right_logical(lo, jnp.uint32(16))


def _unpack_bf16_pairs(xw):
    lo = lax.bitcast_convert_type(lax.shift_left(xw, jnp.uint32(16)), F32)
    hi = lax.bitcast_convert_type(xw & jnp.uint32(0xFFFF0000), F32)
    return jnp.concatenate([lo, hi], axis=1).astype(BF16)


def _post_attn_kernel(om_ref, of_ref, x_ref, wo_ref, gpost_ref, gate_ref, gmoe_ref, sc_ref, sh_ref,
                      wr_ref, br_ref, x1_ref, h2_ref, idx_ref, gts_ref):
    half = om_ref.shape[2]
    o = _dot(om_ref[0], wo_ref[0:half, :]) + _dot(of_ref[0], wo_ref[half:, :])
    x1 = x_ref[0] + gate_ref[0] * _rms(o, gpost_ref[...])
    x1_ref[0] = x1
    h2 = _rms(x1, gmoe_ref[...]) * (1.0 + sc_ref[0]) + sh_ref[0]
    hb = h2.astype(BF16)
    h2_ref[0] = _pack_bf16_pairs(hb)
    logits = _dot_nt(wr_ref[...], hb) + br_ref[...]
    n_e = logits.shape[0]
    eid = lax.broadcasted_iota(jnp.int32, logits.shape, 0)
    vals, idxs = [], []
    for _ in range(TOP_K):
        m = jnp.max(logits, axis=0, keepdims=True)
        ix = jnp.min(jnp.where(logits == m, eid, n_e), axis=0, keepdims=True)
        vals.append(m)
        idxs.append(ix)
        logits = jnp.where(eid == ix, -jnp.inf, logits)
    es = [jnp.exp(vv - vals[0]) for vv in vals]
    den = es[0] + es[1] + es[2] + es[3]
    for kk in range(TOP_K):
        idx_ref[0, kk:kk + 1, :] = idxs[kk]
        gts_ref[0, kk:kk + 1, :] = es[kk] / den


def _post_attn(o_mla, o_fox, x, wo, g_post, gate, g_moe, sc, sh, wr_t, br, ts):
    bsz, s, d = x.shape
    half = o_mla.shape[2]
    tok = lambda width: pl.BlockSpec((1, ts, width), lambda b, i: (b, i, 0))
    per_b = pl.BlockSpec((1, 1, d), lambda b, i: (b, 0, 0))
    full = lambda a: pl.BlockSpec(a.shape, lambda b, i: (0,) * a.ndim)
    k_spec = pl.BlockSpec((1, TOP_K, ts), lambda b, i: (b, 0, i))
    return pl.pallas_call(
        _post_attn_kernel,
        grid=(bsz, s // ts),
        in_specs=[tok(half), tok(half), tok(d), full(wo), full(g_post), per_b, full(g_moe), per_b,
                  per_b, full(wr_t), full(br)],
        out_specs=[tok(d), tok(d // 2), k_spec, k_spec],
        out_shape=[jax.ShapeDtypeStruct((bsz, s, d), F32),
                   jax.ShapeDtypeStruct((bsz, s, d // 2), jnp.uint32),
                   jax.ShapeDtypeStruct((bsz, TOP_K, s), jnp.int32),
                   jax.ShapeDtypeStruct((bsz, TOP_K, s), F32)],
        compiler_params=_cparams(("arbitrary", "arbitrary")),
        name="post_attn",
    )(o_mla, o_fox, x, wo, g_post, gate, g_moe, sc, sh, wr_t, br)


def _route_kernel(idx_ref, tri_ref, rank_ref, cnt_ref, run_ref):
    first = jnp.logical_and(pl.program_id(0) == 0, pl.program_id(1) == 0)

    @pl.when(first)
    def _():
        run_ref[...] = jnp.zeros_like(run_ref)

    tr = idx_ref.shape[2]
    eid = lax.broadcasted_iota(jnp.int32, (N_EXPERTS, tr), 0)
    hot = [eid == idx_ref[0, kk:kk + 1, :] for kk in range(TOP_K)]
    multi = hot[0] | hot[1] | hot[2] | hot[3]
    mf = jnp.where(multi, 1.0, 0.0)
    incl = _dot(mf.astype(BF16), tri_ref[...]) + run_ref[:, 0:1]
    excl = incl - mf
    for kk in range(TOP_K):
        rank_ref[0, kk:kk + 1, :] = jnp.sum(jnp.where(hot[kk], excl, 0.0), axis=0,
                                            keepdims=True).astype(jnp.int32)
    run_ref[...] = jnp.broadcast_to(incl[:, tr - 1:tr], run_ref.shape)
    cnt_ref[...] = run_ref[...].astype(jnp.int32)


def _route(idx_kt, tr):
    bsz, _, s = idx_kt.shape
    tri = (jnp.arange(tr)[:, None] <= jnp.arange(tr)[None, :]).astype(BF16)
    k_spec = pl.BlockSpec((1, TOP_K, tr), lambda b, i: (b, 0, i))
    return pl.pallas_call(
        _route_kernel,
        grid=(bsz, s // tr),
        in_specs=[k_spec, pl.BlockSpec((tr, tr), lambda b, i: (0, 0))],
        out_specs=[k_spec, pl.BlockSpec((N_EXPERTS, LANES), lambda b, i: (0, 0))],
        out_shape=[jax.ShapeDtypeStruct((bsz, TOP_K, s), jnp.int32),
                   jax.ShapeDtypeStruct((N_EXPERTS, LANES), jnp.int32)],
        scratch_shapes=[pltpu.VMEM((N_EXPERTS, LANES), F32)],
        compiler_params=_cparams(("arbitrary", "arbitrary")),
        name="route",
    )(idx_kt, tri)


def _expert_kernel(be_ref, first_ref, nused_ref, gidx_ref, sidx_ref, h_hbm, wgu_ref, bgu_ref, wd_ref,
                   bd_ref, y_hbm, xbuf0, xbuf1, ybuf0, ybuf1, wgu_bf, wd_bf, sem_g, sem_s):
    i = pl.program_id(0)
    n_used = nused_ref[0]
    bm = xbuf0.shape[0]
    ff = wd_ref.shape[1]
    xbufs = (xbuf0, xbuf1)
    ybufs = (ybuf0, ybuf1)

    def gather_start(slot, r, priority=0):
        pltpu.make_async_copy(h_hbm.at[pl.ds(gidx_ref[0, 0, r], 1), :],
                              xbufs[slot].at[pl.ds(r, 1), :], sem_g.at[slot]).start(priority)

    def scatter_start(slot, r, priority=0):
        pltpu.make_async_copy(ybufs[slot].at[pl.ds(r, 1), :],
                              y_hbm.at[pl.ds(sidx_ref[0, 0, r], 1), :],
                              sem_s.at[slot]).start(priority)

    def gather_wait(slot):
        pltpu.make_async_copy(h_hbm.at[pl.ds(0, bm), :], xbufs[slot], sem_g.at[slot]).wait()

    def scatter_wait(slot):
        pltpu.make_async_copy(ybufs[slot], y_hbm.at[pl.ds(0, bm), :], sem_s.at[slot]).wait()

    def rolled(start_fn, slot):
        def body(r, _):
            start_fn(slot, r)
            return 0
        lax.fori_loop(0, bm, body, 0, unroll=8)

    @pl.when(jnp.logical_and(first_ref[i] == 1, i <= n_used))
    def _():
        wgu_bf[...] = wgu_ref[0].astype(BF16)
        wd_bf[...] = wd_ref[0].astype(BF16)

    def step(slot_in):
        slot_cur = 1 - slot_in

        if slot_in == 0:
            @pl.when(i == 0)
            def _():
                ybuf0[...] = jnp.zeros_like(ybuf0)
                ybuf1[...] = jnp.zeros_like(ybuf1)
                rolled(gather_start, 0)

        @pl.when(jnp.logical_and(i >= 1, i <= n_used + 1))
        def _():
            gather_wait(slot_cur)

        @pl.when(jnp.logical_and(i >= 2, i <= n_used + 1))
        def _():
            scatter_wait(slot_cur)

        @pl.when(jnp.logical_and(i >= 1, i <= n_used))
        def _():
            xb = _unpack_bf16_pairs(xbufs[slot_cur][...])
            for r in range(bm):
                scatter_start(slot_in, r, r % 2)
            for r in range(bm):
                gather_start(slot_in, r, r % 2)
            gu = _dot(xb, wgu_bf[...]) + bgu_ref[0]
            g = jnp.minimum(gu[:, :ff], SWIGLU_LIMIT)
            u = jnp.clip(gu[:, ff:], -SWIGLU_LIMIT, SWIGLU_LIMIT)
            glu = g * jax.nn.sigmoid(SWIGLU_ALPHA * g)
            act = ((u + 1.0) * glu).astype(BF16)
            ybufs[slot_cur][...] = _dot(act, wd_bf[...]) + bd_ref[0]

        @pl.when(i == n_used + 1)
        def _():
            rolled(scatter_start, slot_in)
            scatter_wait(slot_in)

    @pl.when(i % 2 == 0)
    def _():
        step(0)

    @pl.when(i % 2 == 1)
    def _():
        step(1)


def _experts(be_step, first_step, n_used, gidx, sidx, h2p, w_gu, b_gu, w_d, b_d, n_out_rows, bm):
    e, d, f2 = w_gu.shape
    ff = w_d.shape[1]
    n_steps = gidx.shape[0]
    smem_blk = pl.BlockSpec((1, 1, bm), lambda i, be, fi, nu: (i, 0, 0), memory_space=pltpu.SMEM)
    return pl.pallas_call(
        _expert_kernel,
        grid_spec=pltpu.PrefetchScalarGridSpec(
            num_scalar_prefetch=3,
            grid=(n_steps,),
            in_specs=[
                smem_blk,
                smem_blk,
                pl.BlockSpec(memory_space=pl.ANY),
                pl.BlockSpec((1, d, f2), lambda i, be, fi, nu: (be[i], 0, 0)),
                pl.BlockSpec((1, 1, f2), lambda i, be, fi, nu: (be[i], 0, 0)),
                pl.BlockSpec((1, ff, d), lambda i, be, fi, nu: (be[i], 0, 0)),
                pl.BlockSpec((1, 1, d), lambda i, be, fi, nu: (be[i], 0, 0)),
            ],
            out_specs=pl.BlockSpec(memory_space=pl.ANY),
            scratch_shapes=[pltpu.VMEM((bm, d // 2), jnp.uint32), pltpu.VMEM((bm, d // 2), jnp.uint32),
                            pltpu.VMEM((bm, d), F32), pltpu.VMEM((bm, d), F32),
                            pltpu.VMEM((d, f2), BF16), pltpu.VMEM((ff, d), BF16),
                            pltpu.SemaphoreType.DMA((2,)), pltpu.SemaphoreType.DMA((2,))],
        ),
        out_shape=jax.ShapeDtypeStruct((n_out_rows, d), F32),
        compiler_params=_cparams(("arbitrary",)),
        name="experts",
    )(be_step, first_step, n_used, gidx, sidx, h2p, w_gu, b_gu.reshape(e, 1, f2), w_d,
      b_d.reshape(e, 1, d))


def _combine_kernel(y0_ref, y1_ref, y2_ref, y3_ref, gts_ref, x1_ref, gpost_ref, gate_ref, o_ref):
    gts = gts_ref[...]
    acc = gts[:, 0:1] * y0_ref[...]
    for kk, y_ref in enumerate((y1_ref, y2_ref, y3_ref), start=1):
        acc = acc + gts[:, kk:kk + 1] * y_ref[...]
    o_ref[0] = x1_ref[0] + gate_ref[0] * _rms(acc, gpost_ref[...])


def _combine(y4, gts, x1, g_post, gate, tt):
    bsz, s, d = x1.shape
    n_t = s // tt
    n_tiles = bsz * n_t
    y_spec = lambda kk: pl.BlockSpec((tt, d), lambda b, i: (kk * n_tiles + b * n_t + i, 0))
    return pl.pallas_call(
        _combine_kernel,
        grid=(bsz, n_t),
        in_specs=[
            y_spec(0), y_spec(1), y_spec(2), y_spec(3),
            pl.BlockSpec((tt, TOP_K), lambda b, i: (b * n_t + i, 0)),
            pl.BlockSpec((1, tt, d), lambda b, i: (b, i, 0)),
            pl.BlockSpec(g_post.shape, lambda b, i: (0, 0)),
            pl.BlockSpec((1, 1, d), lambda b, i: (b, 0, 0)),
        ],
        out_specs=pl.BlockSpec((1, tt, d), lambda b, i: (b, i, 0)),
        out_shape=jax.ShapeDtypeStruct((bsz, s, d), F32),
        compiler_params=_cparams(("arbitrary", "arbitrary")),
        name="combine",
    )(y4, y4, y4, y4, gts, x1, g_post, gate)


def _rope_patterns(positions):
    inv_freq = ROPE_THETA ** (-jnp.arange(0, ROPE_DIM, 2, dtype=F32) / ROPE_DIM)
    ang = positions.astype(F32)[..., None] * inv_freq
    cos, sin = jnp.cos(ang), jnp.sin(ang)
    ones = jnp.ones(positions.shape + (NOPE_DIM,), F32)
    zeros = jnp.zeros(positions.shape + (LANES - NOPE_DIM - ROPE_DIM,), F32)
    cp = jnp.concatenate([ones, cos, cos, zeros], axis=-1)
    sp = jnp.concatenate([0.0 * ones, -sin, sin, zeros], axis=-1)
    return cp, sp


def _prep_mixer_weights(w_in, w_uq, w_ukv, b_forget):
    d = w_in.shape[0]
    o_kr = Q_RANK + KV_RANK
    o_f = o_kr + ROPE_DIM
    hr = ROPE_DIM // 2
    z = lambda n: jnp.zeros((d, n), w_in.dtype)
    kr = w_in[:, o_kr:o_f]
    kr_sw = jnp.concatenate([kr[:, hr:], kr[:, :hr]], axis=1)
    pad = LANES - NOPE_DIM - ROPE_DIM
    w1 = jnp.concatenate([
        w_in[:, :o_kr],
        z(NOPE_DIM), kr, z(pad),
        z(NOPE_DIM), kr_sw, z(pad),
        w_in[:, o_f:o_f + 3 * FOX_WIDTH],
    ], axis=1).astype(BF16)
    wfl = jnp.zeros((FL_ROWS, d), w_in.dtype).at[:FOX_HEADS].set(
        w_in[:, o_f + 3 * FOX_WIDTH:].T).astype(BF16)
    bfg = jnp.zeros((FL_ROWS, 1), F32).at[:FOX_HEADS, 0].set(b_forget)

    wq = w_uq.reshape(Q_RANK, MLA_HEADS, NOPE_DIM + ROPE_DIM)
    nope, rope = wq[..., :NOPE_DIM], wq[..., NOPE_DIM:]
    zq = lambda n: jnp.zeros((Q_RANK, MLA_HEADS, n), w_uq.dtype)
    wqa = jnp.concatenate([nope, rope, zq(pad)], axis=-1).reshape(Q_RANK, -1).astype(BF16)
    wqb = jnp.concatenate([zq(NOPE_DIM), rope[..., hr:], rope[..., :hr], zq(pad)],
                          axis=-1).reshape(Q_RANK, -1).astype(BF16)
    wkv = w_ukv.reshape(KV_RANK, MLA_HEADS, NOPE_DIM + V_DIM)
    wuk = jnp.concatenate([wkv[..., :NOPE_DIM],
                           jnp.zeros((KV_RANK, MLA_HEADS, LANES - NOPE_DIM), w_ukv.dtype)],
                          axis=-1).reshape(KV_RANK, -1).astype(BF16)
    wuv = wkv[..., NOPE_DIM:].reshape(KV_RANK, -1).astype(BF16)
    return w1, wfl, bfg, wqa, wqb, wuk, wuv


def _slot_tables(idx_kt, rank_kt, counts, bm):
    bsz, _, s = idx_kt.shape
    t = bsz * s
    n_blocks = (t * TOP_K) // bm + N_EXPERTS
    n_real = t * TOP_K
    i32 = jnp.int32
    eids = jnp.arange(N_EXPERTS, dtype=i32)

    pcounts = ((counts + bm - 1) // bm) * bm
    incl_mat = (eids[:, None] <= eids[None, :]).astype(i32)
    pends = pcounts @ incl_mat
    pstarts = pends - pcounts
    cstarts = counts @ incl_mat - counts
    total = pends[-1]
    n_used = (total // bm).astype(i32).reshape(1)

    onehot = idx_kt[..., None] == eids
    dest = jnp.sum(jnp.where(onehot, pstarts, 0), axis=-1) + rank_kt
    tok = jnp.arange(t, dtype=i32).reshape(bsz, 1, s)
    kk = jnp.arange(TOP_K, dtype=i32).reshape(1, TOP_K, 1)
    yrow_real = jnp.broadcast_to(kk * t + tok, dest.shape)
    _, rows_sorted = lax.sort((dest.reshape(-1), yrow_real.reshape(-1)), num_keys=1)

    blk0 = jnp.arange(n_blocks, dtype=i32)
    be_blk = jnp.minimum(jnp.sum((pends[None, :] <= (blk0 * bm)[:, None]).astype(i32), axis=1),
                         N_EXPERTS - 1)
    hot_b = be_blk[:, None] == eids[None, :]
    pick = lambda v: jnp.sum(jnp.where(hot_b, v[None, :], 0), axis=1)[:, None]
    local = (blk0 * bm)[:, None] + jnp.arange(bm, dtype=i32)[None, :] - pick(pstarts)
    valid = jnp.logical_and(local < pick(counts), (blk0 < n_used[0])[:, None])
    j = jnp.clip(pick(cstarts) + local, 0, n_real - 1)
    rows = jnp.take(rows_sorted, j.reshape(-1), axis=0).reshape(n_blocks, bm)
    spare = TOP_K * t + jnp.arange(bm, dtype=i32)[None, :]
    yrow = jnp.where(valid, rows, spare)
    src_tok = jnp.where(valid, rows % t, 0)

    zeros2 = jnp.zeros((2, bm), i32)
    spare2 = jnp.broadcast_to(spare, (2, bm))
    gidx = jnp.concatenate([src_tok, zeros2]).reshape(n_blocks + 2, 1, bm)
    sidx = jnp.concatenate([spare2, yrow]).reshape(n_blocks + 2, 1, bm)

    blk = jnp.arange(n_blocks + 2, dtype=i32) - 1
    be = jnp.sum((pends[None, :] <= (jnp.maximum(blk, 0) * bm)[:, None]).astype(i32), axis=1)
    be_step = jnp.minimum(be, N_EXPERTS - 1)
    first_step = jnp.concatenate([jnp.ones((2,), i32),
                                  (be_step[2:] != be_step[1:-1]).astype(i32)])
    return gidx, sidx, be_step, first_step, n_used


def _layer(x, ada, cp, sp, g_attn_pre, g_attn_post, w_in, g_q_norm, w_uq, g_kv_norm, w_ukv,
           b_forget, w_out, g_moe_pre, g_moe_post, w_router, b_router, w_gate_up, b_gate_up,
           w_down, b_down):
    bsz, s, d = x.shape
    t = bsz * s
    ts = min(512, s)
    tq = min(512, s)
    tt = min(512, s)
    tr = min(1024, s)
    bm = 512
    row = lambda v: v.reshape(1, -1)
    sh_a, sc_a, gt_a, sh_m, sc_m, gt_m = [v.reshape(bsz, 1, d) for v in jnp.split(ada, 6, axis=-1)]

    w1, wfl, bfg, wqa, wqb, wuk, wuv = _prep_mixer_weights(w_in, w_uq, w_ukv, b_forget)
    q, k, v, fq, fk, fv, fcum = _pre_attn(x, sc_a, sh_a, row(g_attn_pre), w1, wfl, bfg,
                                          row(g_q_norm), wqa, wqb, row(g_kv_norm), wuk, wuv,
                                          cp, sp, ts)
    o_mla = _attention(q, k, v, None, tq, LANES)
    fpairs = fcum[:, :FOX_HEADS].reshape(bsz, FOX_HEADS // 2, 2, s)
    o_fox = _attention(fq, fk, fv, fpairs, tq, FOX_DIM)

    x1, h2, idx_kt, gts_kt = _post_attn(o_mla, o_fox, x, w_out.astype(BF16), row(g_attn_post), gt_a,
                                        row(g_moe_pre), sc_m, sh_m, w_router.T.astype(BF16),
                                        b_router.reshape(-1, 1), ts)

    rank_kt, cnt = _route(idx_kt, tr)
    gidx, sidx, be_step, first_step, n_used = _slot_tables(idx_kt, rank_kt, cnt[:, 0], bm)
    y4 = _experts(be_step, first_step, n_used, gidx, sidx, h2.reshape(t, d // 2), w_gate_up, b_gate_up,
                  w_down, b_down, TOP_K * t + bm, bm)
    gts = gts_kt.transpose(0, 2, 1).reshape(t, TOP_K)
    return _combine(y4, gts, x1, row(g_moe_post), gt_m, tt)


def kernel(x, c, positions, w_ada, b_ada, g_attn_pre, g_attn_post, w_in, g_q_norm, w_uq, g_kv_norm,
           w_ukv, b_forget, w_out, g_moe_pre, g_moe_post, w_router, b_router, w_gate_up, b_gate_up,
           w_down, b_down):
    cp, sp = _rope_patterns(positions)
    for layer in range(w_ada.shape[0]):
        ada = _ada(c, w_ada[layer], b_ada[layer])
        x = _layer(x, ada, cp, sp, g_attn_pre[layer], g_attn_post[layer], w_in[layer],
                   g_q_norm[layer], w_uq[layer], g_kv_norm[layer], w_ukv[layer], b_forget[layer],
                   w_out[layer], g_moe_pre[layer], g_moe_post[layer], w_router[layer],
                   b_router[layer], w_gate_up[layer], b_gate_up[layer], w_down[layer],
                   b_down[layer])
    return x
```

```python
import functools
import math

import jax
import jax.numpy as jnp
from jax import lax
from jax.experimental import pallas as pl
from jax.experimental.pallas import tpu as pltpu
from jax.experimental.pallas import tpu_sc as plsc

F32 = jnp.float32
BF16 = jnp.bfloat16

MLA_HEADS = 8
NOPE_DIM = 64
ROPE_DIM = 32
V_DIM = 64
Q_RANK = 256
KV_RANK = 128
FOX_HEADS = 8
FOX_DIM = 64
FOX_WIDTH = FOX_HEADS * FOX_DIM
ROPE_THETA = 10000.0
N_EXPERTS = 32
TOP_K = 4
SWIGLU_ALPHA = 1.702
SWIGLU_LIMIT = 7.0
RMS_EPS = 1e-6
NEG_INF = -1e30
LOG2E = math.log2(math.e)
MLA_SCALE = LOG2E / math.sqrt(NOPE_DIM + ROPE_DIM)
FOX_SCALE = LOG2E / math.sqrt(FOX_DIM)

LANES = 128
C_CQ = (0, 256)
C_CKV = (256, 384)
C_KR = (384, 640)
C_FQ = (640, 1152)
C_FK = (1152, 1664)
C_FV = (1664, 2176)
W1_COLS = 2176
FL_ROWS = 16

VMEM_LIMIT = 56 * 1024 * 1024


def _cparams(sem):
    return pltpu.CompilerParams(dimension_semantics=sem, vmem_limit_bytes=VMEM_LIMIT)


def _rms(x, g):
    return x * lax.rsqrt(jnp.mean(x * x, axis=-1, keepdims=True) + RMS_EPS) * g


def _dot(a, b):
    return jnp.dot(a, b, preferred_element_type=F32)


def _dot_nt(a, b):
    return lax.dot_general(a, b, (((1,), (1,)), ((), ())), preferred_element_type=F32)


def _ada_kernel(c_ref, w_ref, b_ref, o_ref):
    c = c_ref[...]
    ca = (c * jax.nn.sigmoid(c)).astype(BF16)
    o_ref[...] = _dot(ca, w_ref[...].astype(BF16)) + b_ref[...]


def _ada(c, w, b):
    bsz, d = c.shape
    n = w.shape[1]
    bn = 1024
    return pl.pallas_call(
        _ada_kernel,
        grid=(n // bn,),
        in_specs=[
            pl.BlockSpec((bsz, d), lambda j: (0, 0)),
            pl.BlockSpec((d, bn), lambda j: (0, j)),
            pl.BlockSpec((1, bn), lambda j: (0, j)),
        ],
        out_specs=pl.BlockSpec((bsz, bn), lambda j: (0, j)),
        out_shape=jax.ShapeDtypeStruct((bsz, n), F32),
        compiler_params=_cparams(("arbitrary",)),
        name="ada",
    )(c, w, b.reshape(1, n))


def _pre_attn_kernel(x_ref, sc_ref, sh_ref, gpre_ref, w1_ref, wfl_ref, bf_ref, gq_ref, wqa_ref,
                     wqb_ref, gkv_ref, wuk_ref, wuv_ref, cp_ref, sp_ref,
                     q_ref, k_ref, v_ref, fq_ref, fk_ref, fv_ref, fc_ref, carry_ref):
    si = pl.program_id(1)
    ts = x_ref.shape[1]
    x = x_ref[0]
    h = _rms(x, gpre_ref[...]) * (1.0 + sc_ref[0]) + sh_ref[0]
    hb = h.astype(BF16)

    def proj(c):
        return _dot(hb, w1_ref[:, c[0]:c[1]])

    cp = cp_ref[0]
    sp = sp_ref[0]

    cqn = _rms(proj(C_CQ), gq_ref[...]).astype(BF16)
    qa = _dot(cqn, wqa_ref[...])
    qb = _dot(cqn, wqb_ref[...])
    for hd in range(MLA_HEADS):
        sl = slice(hd * LANES, (hd + 1) * LANES)
        q_ref[0, :, sl] = ((qa[:, sl] * cp + qb[:, sl] * sp) * MLA_SCALE).astype(BF16)

    kr = proj(C_KR)
    k_rope = kr[:, :LANES] * cp + kr[:, LANES:] * sp
    ckvn = _rms(proj(C_CKV), gkv_ref[...]).astype(BF16)
    kn = _dot(ckvn, wuk_ref[...])
    for hd in range(MLA_HEADS):
        sl = slice(hd * LANES, (hd + 1) * LANES)
        k_ref[0, :, sl] = (kn[:, sl] + k_rope).astype(BF16)
    v_ref[0] = _dot(ckvn, wuv_ref[...]).astype(BF16)

    fq_ref[0] = (proj(C_FQ) * FOX_SCALE).astype(BF16)
    fk_ref[0] = proj(C_FK).astype(BF16)
    fv_ref[0] = proj(C_FV).astype(BF16)

    fl = _dot_nt(wfl_ref[...], hb) + bf_ref[...]
    lf = jnp.minimum(fl, 0.0) - jnp.log1p(jnp.exp(-jnp.abs(fl)))
    r = lax.broadcasted_iota(jnp.int32, (ts, ts), 0)
    c = lax.broadcasted_iota(jnp.int32, (ts, ts), 1)
    tri = (r <= c).astype(BF16)
    p0 = lf.astype(BF16)
    r1 = lf - p0.astype(F32)
    p1 = r1.astype(BF16)
    p2 = (r1 - p1.astype(F32)).astype(BF16)
    cs = _dot(p0, tri) + _dot(p1, tri) + _dot(p2, tri)

    @pl.when(si == 0)
    def _():
        carry_ref[...] = jnp.zeros_like(carry_ref)

    cs = cs + carry_ref[:, 0:1]
    fc_ref[0] = cs * LOG2E
    carry_ref[...] = jnp.broadcast_to(cs[:, ts - 1:ts], carry_ref.shape)


def _pre_attn(x, sc, sh, g_pre, w1, wfl, bfg, g_q, wqa, wqb, g_kv, wuk, wuv, cp, sp, ts):
    bsz, s, d = x.shape
    grid = (bsz, s // ts)
    tok = lambda width: pl.BlockSpec((1, ts, width), lambda b, i: (b, i, 0))
    per_b = pl.BlockSpec((1, 1, d), lambda b, i: (b, 0, 0))
    full = lambda a: pl.BlockSpec(a.shape, lambda b, i: (0,) * a.ndim)
    outs = [
        jax.ShapeDtypeStruct((bsz, s, MLA_HEADS * LANES), BF16),
        jax.ShapeDtypeStruct((bsz, s, MLA_HEADS * LANES), BF16),
        jax.ShapeDtypeStruct((bsz, s, MLA_HEADS * V_DIM), BF16),
        jax.ShapeDtypeStruct((bsz, s, FOX_WIDTH), BF16),
        jax.ShapeDtypeStruct((bsz, s, FOX_WIDTH), BF16),
        jax.ShapeDtypeStruct((bsz, s, FOX_WIDTH), BF16),
        jax.ShapeDtypeStruct((bsz, FL_ROWS, s), F32),
    ]
    out_specs = [tok(MLA_HEADS * LANES), tok(MLA_HEADS * LANES), tok(MLA_HEADS * V_DIM),
                 tok(FOX_WIDTH), tok(FOX_WIDTH), tok(FOX_WIDTH),
                 pl.BlockSpec((1, FL_ROWS, ts), lambda b, i: (b, 0, i))]
    return pl.pallas_call(
        _pre_attn_kernel,
        grid=grid,
        in_specs=[tok(d), per_b, per_b, full(g_pre), full(w1), full(wfl), full(bfg), full(g_q),
                  full(wqa), full(wqb), full(g_kv), full(wuk), full(wuv), tok(LANES), tok(LANES)],
        out_specs=out_specs,
        out_shape=outs,
        scratch_shapes=[pltpu.VMEM((FL_ROWS, LANES), F32)],
        compiler_params=_cparams(("arbitrary", "arbitrary")),
        name="pre_attn",
    )(x, sc, sh, g_pre, w1, wfl, bfg, g_q, wqa, wqb, g_kv, wuk, wuv, cp, sp)


def _attn_kernel(*refs, tq, head_lanes, has_bias):
    if has_bias:
        q_ref, k_ref, v_ref, f_ref, o_ref, acc_ref, m_ref, l_ref = refs
    else:
        q_ref, k_ref, v_ref, o_ref, acc_ref, m_ref, l_ref = refs
        f_ref = None
    qi = pl.program_id(2)
    lane = lax.broadcasted_iota(jnp.int32, (tq, LANES), 1)
    if head_lanes == LANES:
        qs = [q_ref[0, :, 0:LANES], q_ref[0, :, LANES:2 * LANES]]
    else:
        q2 = q_ref[0]
        zero = jnp.zeros_like(q2)
        qs = [jnp.where(lane < head_lanes, q2, zero), jnp.where(lane >= head_lanes, q2, zero)]

    acc_ref[...] = jnp.zeros_like(acc_ref)
    m_ref[...] = jnp.full(m_ref.shape, NEG_INF, F32)
    l_ref[...] = jnp.zeros_like(l_ref)

    def step(off, width, masked):
        v2 = v_ref[0, pl.ds(off, width), :]
        for hd in range(2):
            if head_lanes == LANES:
                kk = k_ref[0, pl.ds(off, width), hd * LANES:(hd + 1) * LANES]
            else:
                kk = k_ref[0, pl.ds(off, width), :]
            s = _dot_nt(qs[hd], kk)
            if has_bias:
                s = s - f_ref[0, 0, hd:hd + 1, pl.ds(off, width)]
            if masked:
                rr = lax.broadcasted_iota(jnp.int32, (tq, width), 0)
                cc = lax.broadcasted_iota(jnp.int32, (tq, width), 1)
                s = jnp.where(cc <= rr, s, NEG_INF)
            m = m_ref[hd]
            m_new = jnp.maximum(m, jnp.max(s, axis=-1, keepdims=True))
            alpha = jnp.exp2(m - m_new)
            p = jnp.exp2(s - jnp.tile(m_new, (1, width // LANES)))
            l_ref[hd] = alpha * l_ref[hd] + jnp.sum(p, axis=-1, keepdims=True)
            m_ref[hd] = m_new
            acc_ref[hd] = alpha * acc_ref[hd] + _dot(p.astype(BF16), v2)

    def wide(j, _):
        step(pl.multiple_of(j * (2 * tq), 2 * tq), 2 * tq, False)
        return 0

    lax.fori_loop(0, qi // 2, wide, 0)

    @pl.when(qi % 2 == 1)
    def _():
        step(pl.multiple_of((qi - 1) * tq, tq), tq, False)

    step(pl.multiple_of(qi * tq, tq), tq, True)
    o0 = acc_ref[0] / l_ref[0]
    o1 = acc_ref[1] / l_ref[1]
    o_ref[0] = jnp.where(lane < V_DIM, o0, o1).astype(o_ref.dtype)


def _attention(q, k, v, fcum, tq, head_lanes):
    bsz, s, _ = q.shape
    n_pairs = v.shape[2] // LANES
    qk_w = 2 * head_lanes
    has_bias = fcum is not None
    in_specs = [
        pl.BlockSpec((1, tq, qk_w), lambda b, p, i: (b, i, p)),
        pl.BlockSpec((1, s, qk_w), lambda b, p, i: (b, 0, p)),
        pl.BlockSpec((1, s, LANES), lambda b, p, i: (b, 0, p)),
    ]
    args = [q, k, v]
    if has_bias:
        in_specs.append(pl.BlockSpec((1, 1, 2, s), lambda b, p, i: (b, p, 0, 0)))
        args.append(fcum)
    return pl.pallas_call(
        functools.partial(_attn_kernel, tq=tq, head_lanes=head_lanes, has_bias=has_bias),
        grid=(bsz, n_pairs, s // tq),
        in_specs=in_specs,
        out_specs=pl.BlockSpec((1, tq, LANES), lambda b, p, i: (b, i, p)),
        out_shape=jax.ShapeDtypeStruct((bsz, s, n_pairs * LANES), BF16),
        scratch_shapes=[pltpu.VMEM((2, tq, LANES), F32), pltpu.VMEM((2, tq, LANES), F32),
                        pltpu.VMEM((2, tq, LANES), F32)],
        compiler_params=_cparams(("arbitrary", "arbitrary", "arbitrary")),
        name="attn_fox" if has_bias else "attn_mla",
    )(*args)


def _pack_bf16_pairs(xb):
    w = xb.shape[1] // 2
    lo = lax.bitcast_convert_type(xb[:, :w].astype(F32), jnp.uint32)
    hi = lax.bitcast_convert_type(xb[:, w:].astype(F32), jnp.uint32)
    return (hi & jnp.uint32(0xFFFF0000)) | lax.shift_right_logical(lo, jnp.uint32(16))


def _unpack_bf16_pairs(xw):
    lo = lax.bitcast_convert_type(lax.shift_left(xw, jnp.uint32(16)), F32)
    hi = lax.bitcast_convert_type(xw & jnp.uint32(0xFFFF0000), F32)
    return jnp.concatenate([lo, hi], axis=1).astype(BF16)


def _post_attn_kernel(om_ref, of_ref, x_ref, wo_ref, gpost_ref, gate_ref, gmoe_ref, sc_ref, sh_ref,
                      wr_ref, br_ref, x1_ref, h2_ref, idx_ref, gts_ref):
    half = om_ref.shape[2]
    o = _dot(om_ref[0], wo_ref[0:half, :]) + _dot(of_ref[0], wo_ref[half:, :])
    x1 = x_ref[0] + gate_ref[0] * _rms(o, gpost_ref[...])
    x1_ref[0] = x1
    h2 = _rms(x1, gmoe_ref[...]) * (1.0 + sc_ref[0]) + sh_ref[0]
    hb = h2.astype(BF16)
    h2_ref[0] = _pack_bf16_pairs(hb)
    logits = _dot_nt(wr_ref[...], hb) + br_ref[...]
    n_e = logits.shape[0]
    eid = lax.broadcasted_iota(jnp.int32, logits.shape, 0)
    vals, idxs = [], []
    for _ in range(TOP_K):
        m = jnp.max(logits, axis=0, keepdims=True)
        ix = jnp.min(jnp.where(logits == m, eid, n_e), axis=0, keepdims=True)
        vals.append(m)
        idxs.append(ix)
        logits = jnp.where(eid == ix, -jnp.inf, logits)
    es = [jnp.exp(vv - vals[0]) for vv in vals]
    den = es[0] + es[1] + es[2] + es[3]
    for kk in range(TOP_K):
        idx_ref[0, kk:kk + 1, :] = idxs[kk]
        gts_ref[0, kk:kk + 1, :] = es[kk] / den


def _post_attn(o_mla, o_fox, x, wo, g_post, gate, g_moe, sc, sh, wr_t, br, ts):
    bsz, s, d = x.shape
    half = o_mla.shape[2]
    tok = lambda width: pl.BlockSpec((1, ts, width), lambda b, i: (b, i, 0))
    per_b = pl.BlockSpec((1, 1, d), lambda b, i: (b, 0, 0))
    full = lambda a: pl.BlockSpec(a.shape, lambda b, i: (0,) * a.ndim)
    k_spec = pl.BlockSpec((1, TOP_K, ts), lambda b, i: (b, 0, i))
    return pl.pallas_call(
        _post_attn_kernel,
        grid=(bsz, s // ts),
        in_specs=[tok(half), tok(half), tok(d), full(wo), full(g_post), per_b, full(g_moe), per_b,
                  per_b, full(wr_t), full(br)],
        out_specs=[tok(d), tok(d // 2), k_spec, k_spec],
        out_shape=[jax.ShapeDtypeStruct((bsz, s, d), F32),
                   jax.ShapeDtypeStruct((bsz, s, d // 2), jnp.uint32),
                   jax.ShapeDtypeStruct((bsz, TOP_K, s), jnp.int32),
                   jax.ShapeDtypeStruct((bsz, TOP_K, s), F32)],
        compiler_params=_cparams(("arbitrary", "arbitrary")),
        name="post_attn",
    )(o_mla, o_fox, x, wo, g_post, gate, g_moe, sc, sh, wr_t, br)


def _route_kernel(idx_ref, tri_ref, rank_ref, cnt_ref, run_ref):
    first = jnp.logical_and(pl.program_id(0) == 0, pl.program_id(1) == 0)

    @pl.when(first)
    def _():
        run_ref[...] = jnp.zeros_like(run_ref)

    tr = idx_ref.shape[2]
    eid = lax.broadcasted_iota(jnp.int32, (N_EXPERTS, tr), 0)
    hot = [eid == idx_ref[0, kk:kk + 1, :] for kk in range(TOP_K)]
    multi = hot[0] | hot[1] | hot[2] | hot[3]
    mf = jnp.where(multi, 1.0, 0.0)
    incl = _dot(mf.astype(BF16), tri_ref[...]) + run_ref[:, 0:1]
    excl = incl - mf
    for kk in range(TOP_K):
        rank_ref[0, kk:kk + 1, :] = jnp.sum(jnp.where(hot[kk], excl, 0.0), axis=0,
                                            keepdims=True).astype(jnp.int32)
    run_ref[...] = jnp.broadcast_to(incl[:, tr - 1:tr], run_ref.shape)
    cnt_ref[...] = run_ref[...].astype(jnp.int32)


def _route(idx_kt, tr):
    bsz, _, s = idx_kt.shape
    tri = (jnp.arange(tr)[:, None] <= jnp.arange(tr)[None, :]).astype(BF16)
    k_spec = pl.BlockSpec((1, TOP_K, tr), lambda b, i: (b, 0, i))
    return pl.pallas_call(
        _route_kernel,
        grid=(bsz, s // tr),
        in_specs=[k_spec, pl.BlockSpec((tr, tr), lambda b, i: (0, 0))],
        out_specs=[k_spec, pl.BlockSpec((N_EXPERTS, LANES), lambda b, i: (0, 0))],
        out_shape=[jax.ShapeDtypeStruct((bsz, TOP_K, s), jnp.int32),
                   jax.ShapeDtypeStruct((N_EXPERTS, LANES), jnp.int32)],
        scratch_shapes=[pltpu.VMEM((N_EXPERTS, LANES), F32)],
        compiler_params=_cparams(("arbitrary", "arbitrary")),
        name="route",
    )(idx_kt, tri)


SC_CORES = 2
SC_SUBCORES = 16
SC_CHUNK = 64


def _sc_gather(table, idx):
    m = idx.shape[0]
    width = table.shape[1]
    n_workers = SC_CORES * SC_SUBCORES
    per_worker = m // n_workers
    n_chunks = per_worker // SC_CHUNK
    assert n_chunks * SC_CHUNK * n_workers == m
    mesh = plsc.VectorSubcoreMesh(core_axis_name="c", subcore_axis_name="s")

    @functools.partial(
        pl.kernel, mesh=mesh,
        out_type=jax.ShapeDtypeStruct((m, width), table.dtype),
        scratch_types=[pltpu.VMEM((SC_CHUNK,), jnp.int32),
                       pltpu.VMEM((SC_CHUNK, width), table.dtype),
                       pltpu.SemaphoreType.DMA],
        name="sc_gather",
    )
    def gather(table_hbm, idx_hbm, out_hbm, idx_v, rows_v, sem):
        wid = lax.axis_index("s") * SC_CORES + lax.axis_index("c")
        base = wid * per_worker

        @pl.loop(0, n_chunks)
        def _(j):
            off = pl.multiple_of(base + j * SC_CHUNK, SC_CHUNK)
            pltpu.sync_copy(idx_hbm.at[pl.ds(off, SC_CHUNK)], idx_v)
            pltpu.async_copy(table_hbm.at[idx_v], rows_v, sem).wait()
            pltpu.sync_copy(rows_v, out_hbm.at[pl.ds(off, SC_CHUNK)])

    return gather(table, idx)


def _expert_mlp(xb, wgu_bf, bgu_ref, wd_bf, bd_ref):
    ff = wd_bf.shape[0]
    gu = _dot(xb, wgu_bf[...]) + bgu_ref[0]
    g = jnp.minimum(gu[:, :ff], SWIGLU_LIMIT)
    u = jnp.clip(gu[:, ff:], -SWIGLU_LIMIT, SWIGLU_LIMIT)
    glu = g * jax.nn.sigmoid(SWIGLU_ALPHA * g)
    act = ((u + 1.0) * glu).astype(BF16)
    return _dot(act, wd_bf[...]) + bd_ref[0]


def _expert_pregathered_kernel(be_ref, first_ref, n_ref, sidx_ref, x_ref, wgu_ref, bgu_ref, wd_ref,
                               bd_ref, yin_hbm, y_hbm, ybuf0, ybuf1, wgu_bf, wd_bf, sem_s):
    del yin_hbm
    i = pl.program_id(0)
    n = n_ref[0]
    bm = ybuf0.shape[0]
    ybufs = (ybuf0, ybuf1)

    def scatter_start(slot, r, priority=0):
        pltpu.make_async_copy(ybufs[slot].at[pl.ds(r, 1), :],
                              y_hbm.at[pl.ds(sidx_ref[0, 0, r], 1), :],
                              sem_s.at[slot]).start(priority)

    def scatter_wait(slot):
        pltpu.make_async_copy(ybufs[slot], y_hbm.at[pl.ds(0, bm), :], sem_s.at[slot]).wait()

    @pl.when(jnp.logical_and(first_ref[i] == 1, i < n))
    def _():
        wgu_bf[...] = wgu_ref[0].astype(BF16)
        wd_bf[...] = wd_ref[0].astype(BF16)

    def step(par):
        if par == 0:
            @pl.when(i == 0)
            def _():
                ybuf1[...] = jnp.zeros_like(ybuf1)

        @pl.when(jnp.logical_and(i >= 1, i <= n))
        def _():
            scatter_wait(par)

        @pl.when(i < n)
        def _():
            xb = _unpack_bf16_pairs(x_ref[...])
            for r in range(bm):
                scatter_start(1 - par, r, r % 2)
            ybufs[par][...] = _expert_mlp(xb, wgu_bf, bgu_ref, wd_bf, bd_ref)

        @pl.when(i == n)
        def _():
            def body(r, _):
                scatter_start(1 - par, r)
                return 0
            lax.fori_loop(0, bm, body, 0, unroll=8)
            scatter_wait(1 - par)

    @pl.when(i % 2 == 0)
    def _():
        step(0)

    @pl.when(i % 2 == 1)
    def _():
        step(1)


def _experts_pregathered(be_step, first_step, n_real, sidx, xs, w_gu, b_gu, w_d, b_d, y4, bm):
    e, d, f2 = w_gu.shape
    ff = w_d.shape[1]
    n_steps = sidx.shape[0]
    n_xblocks = xs.shape[0] // bm
    n_prefetch = 3
    return pl.pallas_call(
        _expert_pregathered_kernel,
        grid_spec=pltpu.PrefetchScalarGridSpec(
            num_scalar_prefetch=n_prefetch,
            grid=(n_steps,),
            in_specs=[
                pl.BlockSpec((1, 1, bm), lambda i, be, fi, nu: (i, 0, 0), memory_space=pltpu.SMEM),
                pl.BlockSpec((bm, d // 2), lambda i, be, fi, nu: (jnp.minimum(i, n_xblocks - 1), 0)),
                pl.BlockSpec((1, d, f2), lambda i, be, fi, nu: (be[i], 0, 0)),
                pl.BlockSpec((1, 1, f2), lambda i, be, fi, nu: (be[i], 0, 0)),
                pl.BlockSpec((1, ff, d), lambda i, be, fi, nu: (be[i], 0, 0)),
                pl.BlockSpec((1, 1, d), lambda i, be, fi, nu: (be[i], 0, 0)),
                pl.BlockSpec(memory_space=pl.ANY),
            ],
            out_specs=pl.BlockSpec(memory_space=pl.ANY),
            scratch_shapes=[pltpu.VMEM((bm, d), F32), pltpu.VMEM((bm, d), F32),
                            pltpu.VMEM((d, f2), BF16), pltpu.VMEM((ff, d), BF16),
                            pltpu.SemaphoreType.DMA((2,))],
        ),
        out_shape=jax.ShapeDtypeStruct(y4.shape, y4.dtype),
        input_output_aliases={n_prefetch + 6: 0},
        compiler_params=_cparams(("arbitrary",)),
        name="experts_pregathered",
    )(be_step, first_step, n_real, sidx, xs, w_gu, b_gu.reshape(e, 1, f2), w_d, b_d.reshape(e, 1, d),
      y4)

def _expert_kernel(be_ref, first_ref, nused_ref, gidx_ref, sidx_ref, h_hbm, wgu_ref, bgu_ref, wd_ref,
                   bd_ref, y_hbm, xbuf0, xbuf1, ybuf0, ybuf1, wgu_bf, wd_bf, sem_g, sem_s):
    i = pl.program_id(0)
    n_used = nused_ref[0]
    bm = xbuf0.shape[0]
    ff = wd_ref.shape[1]
    xbufs = (xbuf0, xbuf1)
    ybufs = (ybuf0, ybuf1)

    def gather_start(slot, r, priority=0):
        pltpu.make_async_copy(h_hbm.at[pl.ds(gidx_ref[0, 0, r], 1), :],
                              xbufs[slot].at[pl.ds(r, 1), :], sem_g.at[slot]).start(priority)

    def scatter_start(slot, r, priority=0):
        pltpu.make_async_copy(ybufs[slot].at[pl.ds(r, 1), :],
                              y_hbm.at[pl.ds(sidx_ref[0, 0, r], 1), :],
                              sem_s.at[slot]).start(priority)

    def gather_wait(slot):
        pltpu.make_async_copy(h_hbm.at[pl.ds(0, bm), :], xbufs[slot], sem_g.at[slot]).wait()

    def scatter_wait(slot):
        pltpu.make_async_copy(ybufs[slot], y_hbm.at[pl.ds(0, bm), :], sem_s.at[slot]).wait()

    def rolled(start_fn, slot):
        def body(r, _):
            start_fn(slot, r)
            return 0
        lax.fori_loop(0, bm, body, 0, unroll=8)

    @pl.when(jnp.logical_and(first_ref[i] == 1, i <= n_used))
    def _():
        wgu_bf[...] = wgu_ref[0].astype(BF16)
        wd_bf[...] = wd_ref[0].astype(BF16)

    def step(slot_in):
        slot_cur = 1 - slot_in

        if slot_in == 0:
            @pl.when(i == 0)
            def _():
                ybuf0[...] = jnp.zeros_like(ybuf0)
                ybuf1[...] = jnp.zeros_like(ybuf1)
                rolled(gather_start, 0)

        @pl.when(jnp.logical_and(i >= 1, i <= n_used + 1))
        def _():
            gather_wait(slot_cur)

        @pl.when(jnp.logical_and(i >= 2, i <= n_used + 1))
        def _():
            scatter_wait(slot_cur)

        @pl.when(jnp.logical_and(i >= 1, i <= n_used))
        def _():
            xb = _unpack_bf16_pairs(xbufs[slot_cur][...])
            for r in range(bm):
                scatter_start(slot_in, r, r % 2)
            for r in range(bm):
                gather_start(slot_in, r, r % 2)
            ybufs[slot_cur][...] = _expert_mlp(xb, wgu_bf, bgu_ref, wd_bf, bd_ref)

        @pl.when(i == n_used + 1)
        def _():
            rolled(scatter_start, slot_in)
            scatter_wait(slot_in)

    @pl.when(i % 2 == 0)
    def _():
        step(0)

    @pl.when(i % 2 == 1)
    def _():
        step(1)


def _experts(be_step, first_step, n_used, gidx, sidx, h2p, w_gu, b_gu, w_d, b_d, n_out_rows, bm):
    e, d, f2 = w_gu.shape
    ff = w_d.shape[1]
    n_steps = gidx.shape[0]
    smem_blk = pl.BlockSpec((1, 1, bm), lambda i, be, fi, nu: (i, 0, 0), memory_space=pltpu.SMEM)
    return pl.pallas_call(
        _expert_kernel,
        grid_spec=pltpu.PrefetchScalarGridSpec(
            num_scalar_prefetch=3,
            grid=(n_steps,),
            in_specs=[
                smem_blk,
                smem_blk,
                pl.BlockSpec(memory_space=pl.ANY),
                pl.BlockSpec((1, d, f2), lambda i, be, fi, nu: (be[i], 0, 0)),
                pl.BlockSpec((1, 1, f2), lambda i, be, fi, nu: (be[i], 0, 0)),
                pl.BlockSpec((1, ff, d), lambda i, be, fi, nu: (be[i], 0, 0)),
                pl.BlockSpec((1, 1, d), lambda i, be, fi, nu: (be[i], 0, 0)),
            ],
            out_specs=pl.BlockSpec(memory_space=pl.ANY),
            scratch_shapes=[pltpu.VMEM((bm, d // 2), jnp.uint32), pltpu.VMEM((bm, d // 2), jnp.uint32),
                            pltpu.VMEM((bm, d), F32), pltpu.VMEM((bm, d), F32),
                            pltpu.VMEM((d, f2), BF16), pltpu.VMEM((ff, d), BF16),
                            pltpu.SemaphoreType.DMA((2,)), pltpu.SemaphoreType.DMA((2,))],
        ),
        out_shape=jax.ShapeDtypeStruct((n_out_rows, d), F32),
        compiler_params=_cparams(("arbitrary",)),
        name="experts",
    )(be_step, first_step, n_used, gidx, sidx, h2p, w_gu, b_gu.reshape(e, 1, f2), w_d,
      b_d.reshape(e, 1, d))


def _combine_kernel(y0_ref, y1_ref, y2_ref, y3_ref, gts_ref, x1_ref, gpost_ref, gate_ref, o_ref):
    gts = gts_ref[...]
    acc = gts[:, 0:1] * y0_ref[...]
    for kk, y_ref in enumerate((y1_ref, y2_ref, y3_ref), start=1):
        acc = acc + gts[:, kk:kk + 1] * y_ref[...]
    o_ref[0] = x1_ref[0] + gate_ref[0] * _rms(acc, gpost_ref[...])


def _combine(y4, gts, x1, g_post, gate, tt):
    bsz, s, d = x1.shape
    n_t = s // tt
    n_tiles = bsz * n_t
    y_spec = lambda kk: pl.BlockSpec((tt, d), lambda b, i: (kk * n_tiles + b * n_t + i, 0))
    return pl.pallas_call(
        _combine_kernel,
        grid=(bsz, n_t),
        in_specs=[
            y_spec(0), y_spec(1), y_spec(2), y_spec(3),
            pl.BlockSpec((tt, TOP_K), lambda b, i: (b * n_t + i, 0)),
            pl.BlockSpec((1, tt, d), lambda b, i: (b, i, 0)),
            pl.BlockSpec(g_post.shape, lambda b, i: (0, 0)),
            pl.BlockSpec((1, 1, d), lambda b, i: (b, 0, 0)),
        ],
        out_specs=pl.BlockSpec((1, tt, d), lambda b, i: (b, i, 0)),
        out_shape=jax.ShapeDtypeStruct((bsz, s, d), F32),
        compiler_params=_cparams(("arbitrary", "arbitrary")),
        name="combine",
    )(y4, y4, y4, y4, gts, x1, g_post, gate)


def _rope_patterns(positions):
    inv_freq = ROPE_THETA ** (-jnp.arange(0, ROPE_DIM, 2, dtype=F32) / ROPE_DIM)
    ang = positions.astype(F32)[..., None] * inv_freq
    cos, sin = jnp.cos(ang), jnp.sin(ang)
    ones = jnp.ones(positions.shape + (NOPE_DIM,), F32)
    zeros = jnp.zeros(positions.shape + (LANES - NOPE_DIM - ROPE_DIM,), F32)
    cp = jnp.concatenate([ones, cos, cos, zeros], axis=-1)
    sp = jnp.concatenate([0.0 * ones, -sin, sin, zeros], axis=-1)
    return cp, sp


def _prep_mixer_weights(w_in, w_uq, w_ukv, b_forget):
    d = w_in.shape[0]
    o_kr = Q_RANK + KV_RANK
    o_f = o_kr + ROPE_DIM
    hr = ROPE_DIM // 2
    z = lambda n: jnp.zeros((d, n), w_in.dtype)
    kr = w_in[:, o_kr:o_f]
    kr_sw = jnp.concatenate([kr[:, hr:], kr[:, :hr]], axis=1)
    pad = LANES - NOPE_DIM - ROPE_DIM
    w1 = jnp.concatenate([
        w_in[:, :o_kr],
        z(NOPE_DIM), kr, z(pad),
        z(NOPE_DIM), kr_sw, z(pad),
        w_in[:, o_f:o_f + 3 * FOX_WIDTH],
    ], axis=1).astype(BF16)
    wfl = jnp.zeros((FL_ROWS, d), w_in.dtype).at[:FOX_HEADS].set(
        w_in[:, o_f + 3 * FOX_WIDTH:].T).astype(BF16)
    bfg = jnp.zeros((FL_ROWS, 1), F32).at[:FOX_HEADS, 0].set(b_forget)

    wq = w_uq.reshape(Q_RANK, MLA_HEADS, NOPE_DIM + ROPE_DIM)
    nope, rope = wq[..., :NOPE_DIM], wq[..., NOPE_DIM:]
    zq = lambda n: jnp.zeros((Q_RANK, MLA_HEADS, n), w_uq.dtype)
    wqa = jnp.concatenate([nope, rope, zq(pad)], axis=-1).reshape(Q_RANK, -1).astype(BF16)
    wqb = jnp.concatenate([zq(NOPE_DIM), rope[..., hr:], rope[..., :hr], zq(pad)],
                          axis=-1).reshape(Q_RANK, -1).astype(BF16)
    wkv = w_ukv.reshape(KV_RANK, MLA_HEADS, NOPE_DIM + V_DIM)
    wuk = jnp.concatenate([wkv[..., :NOPE_DIM],
                           jnp.zeros((KV_RANK, MLA_HEADS, LANES - NOPE_DIM), w_ukv.dtype)],
                          axis=-1).reshape(KV_RANK, -1).astype(BF16)
    wuv = wkv[..., NOPE_DIM:].reshape(KV_RANK, -1).astype(BF16)
    return w1, wfl, bfg, wqa, wqb, wuk, wuv


def _slot_tables(idx_kt, rank_kt, counts, bm):
    bsz, _, s = idx_kt.shape
    t = bsz * s
    n_blocks = (t * TOP_K) // bm + N_EXPERTS
    n_real = t * TOP_K
    i32 = jnp.int32
    eids = jnp.arange(N_EXPERTS, dtype=i32)

    pcounts = ((counts + bm - 1) // bm) * bm
    incl_mat = (eids[:, None] <= eids[None, :]).astype(i32)
    pends = pcounts @ incl_mat
    pstarts = pends - pcounts
    cstarts = counts @ incl_mat - counts
    total = pends[-1]
    n_used = (total // bm).astype(i32).reshape(1)

    onehot = idx_kt[..., None] == eids
    dest = jnp.sum(jnp.where(onehot, pstarts, 0), axis=-1) + rank_kt
    tok = jnp.arange(t, dtype=i32).reshape(bsz, 1, s)
    kk = jnp.arange(TOP_K, dtype=i32).reshape(1, TOP_K, 1)
    yrow_real = jnp.broadcast_to(kk * t + tok, dest.shape)
    _, rows_sorted = lax.sort((dest.reshape(-1), yrow_real.reshape(-1)), num_keys=1)

    blk0 = jnp.arange(n_blocks, dtype=i32)
    be_blk = jnp.minimum(jnp.sum((pends[None, :] <= (blk0 * bm)[:, None]).astype(i32), axis=1),
                         N_EXPERTS - 1)
    hot_b = be_blk[:, None] == eids[None, :]
    pick = lambda v: jnp.sum(jnp.where(hot_b, v[None, :], 0), axis=1)[:, None]
    local = (blk0 * bm)[:, None] + jnp.arange(bm, dtype=i32)[None, :] - pick(pstarts)
    valid = jnp.logical_and(local < pick(counts), (blk0 < n_used[0])[:, None])
    j = jnp.clip(pick(cstarts) + local, 0, n_real - 1)
    rows = jnp.take(rows_sorted, j.reshape(-1), axis=0).reshape(n_blocks, bm)
    spare = TOP_K * t + jnp.arange(bm, dtype=i32)[None, :]
    yrow = jnp.where(valid, rows, spare)
    src_tok = jnp.where(valid, rows % t, 0)

    return src_tok, yrow, be_blk, n_used[0]


def _first_flags(be_step):
    return jnp.concatenate([jnp.ones((1,), jnp.int32),
                            (be_step[1:] != be_step[:-1]).astype(jnp.int32)])


def _moe_blocks(src_tok, yrow, be_blk, n_used, h2p, w_gu, b_gu, w_d, b_d, n_out_rows, bm, nb_fused):
    i32 = jnp.int32
    n_blocks = src_tok.shape[0]
    spare = (n_out_rows - bm) + jnp.arange(bm, dtype=i32)[None, :]
    zeros2 = jnp.zeros((2, bm), i32)

    gidx = jnp.concatenate([src_tok[:nb_fused], zeros2]).reshape(nb_fused + 2, 1, bm)
    sidx = jnp.concatenate([spare, spare, yrow[:nb_fused]]).reshape(nb_fused + 2, 1, bm)
    be_a = be_blk[jnp.clip(jnp.arange(nb_fused + 2, dtype=i32) - 1, 0, nb_fused - 1)]
    n_a = jnp.minimum(n_used, nb_fused).astype(i32).reshape(1)
    y4 = _experts(be_a, _first_flags(be_a), n_a, gidx, sidx, h2p, w_gu, b_gu, w_d, b_d,
                  n_out_rows, bm)

    nb_rest = n_blocks - nb_fused
    xs = _sc_gather(h2p, src_tok[nb_fused:].reshape(-1))
    sidx_b = jnp.concatenate([spare, yrow[nb_fused:]]).reshape(nb_rest + 1, 1, bm)
    be_b = be_blk[nb_fused + jnp.minimum(jnp.arange(nb_rest + 1, dtype=i32), nb_rest - 1)]
    n_b = jnp.clip(n_used - nb_fused, 0, nb_rest).astype(i32).reshape(1)
    return _experts_pregathered(be_b, _first_flags(be_b), n_b, sidx_b, xs, w_gu, b_gu, w_d, b_d,
                                y4, bm)


def _layer(x, ada, cp, sp, g_attn_pre, g_attn_post, w_in, g_q_norm, w_uq, g_kv_norm, w_ukv,
           b_forget, w_out, g_moe_pre, g_moe_post, w_router, b_router, w_gate_up, b_gate_up,
           w_down, b_down):
    bsz, s, d = x.shape
    t = bsz * s
    ts = min(512, s)
    tq = min(512, s)
    tt = min(512, s)
    tr = min(1024, s)
    bm = 512
    row = lambda v: v.reshape(1, -1)
    sh_a, sc_a, gt_a, sh_m, sc_m, gt_m = [v.reshape(bsz, 1, d) for v in jnp.split(ada, 6, axis=-1)]

    w1, wfl, bfg, wqa, wqb, wuk, wuv = _prep_mixer_weights(w_in, w_uq, w_ukv, b_forget)
    q, k, v, fq, fk, fv, fcum = _pre_attn(x, sc_a, sh_a, row(g_attn_pre), w1, wfl, bfg,
                                          row(g_q_norm), wqa, wqb, row(g_kv_norm), wuk, wuv,
                                          cp, sp, ts)
    o_mla = _attention(q, k, v, None, tq, LANES)
    fpairs = fcum[:, :FOX_HEADS].reshape(bsz, FOX_HEADS // 2, 2, s)
    o_fox = _attention(fq, fk, fv, fpairs, tq, FOX_DIM)

    x1, h2, idx_kt, gts_kt = _post_attn(o_mla, o_fox, x, w_out.astype(BF16), row(g_attn_post), gt_a,
                                        row(g_moe_pre), sc_m, sh_m, w_router.T.astype(BF16),
                                        b_router.reshape(-1, 1), ts)

    rank_kt, cnt = _route(idx_kt, tr)
    src_tok, yrow, be_blk, n_used = _slot_tables(idx_kt, rank_kt, cnt[:, 0], bm)
    n_blocks = src_tok.shape[0]
    sc_blocks = (SC_CORES * SC_SUBCORES * SC_CHUNK) // bm
    nb_fused = n_blocks - ((3 * n_blocks // 4) // sc_blocks) * sc_blocks
    y4 = _moe_blocks(src_tok, yrow, be_blk, n_used, h2.reshape(t, d // 2), w_gate_up, b_gate_up,
                     w_down, b_down, TOP_K * t + bm, bm, nb_fused)
    gts = gts_kt.transpose(0, 2, 1).reshape(t, TOP_K)
    return _combine(y4, gts, x1, row(g_moe_post), gt_m, tt)


def kernel(x, c, positions, w_ada, b_ada, g_attn_pre, g_attn_post, w_in, g_q_norm, w_uq, g_kv_norm,
           w_ukv, b_forget, w_out, g_moe_pre, g_moe_post, w_router, b_router, w_gate_up, b_gate_up,
           w_down, b_down):
    cp, sp = _rope_patterns(positions)
    for layer in range(w_ada.shape[0]):
        ada = _ada(c, w_ada[layer], b_ada[layer])
        x = _layer(x, ada, cp, sp, g_attn_pre[layer], g_attn_post[layer], w_in[layer],
                   g_q_norm[layer], w_uq[layer], g_kv_norm[layer], w_ukv[layer], b_forget[layer],
                   w_out[layer], g_moe_pre[layer], g_moe_post[layer], w_router[layer],
                   b_router[layer], w_gate_up[layer], b_gate_up[layer], w_down[layer],
                   b_down[layer])
    return x
```

```python
import functools
import math

import jax
import jax.numpy as jnp
from jax import lax
from jax.experimental import pallas as pl
from jax.experimental.pallas import tpu as pltpu
from jax.experimental.pallas import tpu_sc as plsc

F32 = jnp.float32
BF16 = jnp.bfloat16

MLA_HEADS = 8
NOPE_DIM = 64
ROPE_DIM = 32
V_DIM = 64
Q_RANK = 256
KV_RANK = 128
FOX_HEADS = 8
FOX_DIM = 64
FOX_WIDTH = FOX_HEADS * FOX_DIM
ROPE_THETA = 10000.0
N_EXPERTS = 32
TOP_K = 4
SWIGLU_ALPHA = 1.702
SWIGLU_LIMIT = 7.0
RMS_EPS = 1e-6
NEG_INF = -1e30
LOG2E = math.log2(math.e)
MLA_SCALE = LOG2E / math.sqrt(NOPE_DIM + ROPE_DIM)
FOX_SCALE = LOG2E / math.sqrt(FOX_DIM)

LANES = 128
C_CQ = (0, 256)
C_CKV = (256, 384)
C_KR = (384, 640)
C_FQ = (640, 1152)
C_FK = (1152, 1664)
C_FV = (1664, 2176)
W1_COLS = 2176
FL_ROWS = 16

VMEM_LIMIT = 56 * 1024 * 1024


def _cparams(sem):
    return pltpu.CompilerParams(dimension_semantics=sem, vmem_limit_bytes=VMEM_LIMIT)


def _rms(x, g):
    return x * lax.rsqrt(jnp.mean(x * x, axis=-1, keepdims=True) + RMS_EPS) * g


def _dot(a, b):
    return jnp.dot(a, b, preferred_element_type=F32)


def _dot_nt(a, b):
    return lax.dot_general(a, b, (((1,), (1,)), ((), ())), preferred_element_type=F32)


def _ada_kernel(c_ref, w_ref, b_ref, o_ref):
    c = c_ref[...]
    ca = (c * jax.nn.sigmoid(c)).astype(BF16)
    o_ref[...] = _dot(ca, w_ref[...].astype(BF16)) + b_ref[...]


def _ada(c, w, b):
    bsz, d = c.shape
    n = w.shape[1]
    bn = 1024
    return pl.pallas_call(
        _ada_kernel,
        grid=(n // bn,),
        in_specs=[
            pl.BlockSpec((bsz, d), lambda j: (0, 0)),
            pl.BlockSpec((d, bn), lambda j: (0, j)),
            pl.BlockSpec((1, bn), lambda j: (0, j)),
        ],
        out_specs=pl.BlockSpec((bsz, bn), lambda j: (0, j)),
        out_shape=jax.ShapeDtypeStruct((bsz, n), F32),
        compiler_params=_cparams(("arbitrary",)),
        name="ada",
    )(c, w, b.reshape(1, n))


def _pre_attn_kernel(x_ref, sc_ref, sh_ref, gpre_ref, w1_ref, wfl_ref, bf_ref, gq_ref, wqa_ref,
                     wqb_ref, gkv_ref, wuk_ref, wuv_ref, cp_ref, sp_ref,
                     q_ref, k_ref, v_ref, fq_ref, fk_ref, fv_ref, fc_ref, carry_ref):
    si = pl.program_id(1)
    ts = x_ref.shape[1]
    x = x_ref[0]
    h = _rms(x, gpre_ref[...]) * (1.0 + sc_ref[0]) + sh_ref[0]
    hb = h.astype(BF16)

    def proj(c):
        return _dot(hb, w1_ref[:, c[0]:c[1]])

    cp = cp_ref[0]
    sp = sp_ref[0]

    cqn = _rms(proj(C_CQ), gq_ref[...]).astype(BF16)
    qa = _dot(cqn, wqa_ref[...])
    qb = _dot(cqn, wqb_ref[...])
    for hd in range(MLA_HEADS):
        sl = slice(hd * LANES, (hd + 1) * LANES)
        q_ref[0, :, sl] = ((qa[:, sl] * cp + qb[:, sl] * sp) * MLA_SCALE).astype(BF16)

    kr = proj(C_KR)
    k_rope = kr[:, :LANES] * cp + kr[:, LANES:] * sp
    ckvn = _rms(proj(C_CKV), gkv_ref[...]).astype(BF16)
    kn = _dot(ckvn, wuk_ref[...])
    for hd in range(MLA_HEADS):
        sl = slice(hd * LANES, (hd + 1) * LANES)
        k_ref[0, :, sl] = (kn[:, sl] + k_rope).astype(BF16)
    v_ref[0] = _dot(ckvn, wuv_ref[...]).astype(BF16)

    fq_ref[0] = (proj(C_FQ) * FOX_SCALE).astype(BF16)
    fk_ref[0] = proj(C_FK).astype(BF16)
    fv_ref[0] = proj(C_FV).astype(BF16)

    fl = _dot_nt(wfl_ref[...], hb) + bf_ref[...]
    lf = jnp.minimum(fl, 0.0) - jnp.log1p(jnp.exp(-jnp.abs(fl)))
    r = lax.broadcasted_iota(jnp.int32, (ts, ts), 0)
    c = lax.broadcasted_iota(jnp.int32, (ts, ts), 1)
    tri = (r <= c).astype(BF16)
    p0 = lf.astype(BF16)
    r1 = lf - p0.astype(F32)
    p1 = r1.astype(BF16)
    p2 = (r1 - p1.astype(F32)).astype(BF16)
    cs = _dot(p0, tri) + _dot(p1, tri) + _dot(p2, tri)

    @pl.when(si == 0)
    def _():
        carry_ref[...] = jnp.zeros_like(carry_ref)

    cs = cs + carry_ref[:, 0:1]
    fc_ref[0] = cs * LOG2E
    carry_ref[...] = jnp.broadcast_to(cs[:, ts - 1:ts], carry_ref.shape)


def _pre_attn(x, sc, sh, g_pre, w1, wfl, bfg, g_q, wqa, wqb, g_kv, wuk, wuv, cp, sp, ts):
    bsz, s, d = x.shape
    grid = (bsz, s // ts)
    tok = lambda width: pl.BlockSpec((1, ts, width), lambda b, i: (b, i, 0))
    per_b = pl.BlockSpec((1, 1, d), lambda b, i: (b, 0, 0))
    full = lambda a: pl.BlockSpec(a.shape, lambda b, i: (0,) * a.ndim)
    outs = [
        jax.ShapeDtypeStruct((bsz, s, MLA_HEADS * LANES), BF16),
        jax.ShapeDtypeStruct((bsz, s, MLA_HEADS * LANES), BF16),
        jax.ShapeDtypeStruct((bsz, s, MLA_HEADS * V_DIM), BF16),
        jax.ShapeDtypeStruct((bsz, s, FOX_WIDTH), BF16),
        jax.ShapeDtypeStruct((bsz, s, FOX_WIDTH), BF16),
        jax.ShapeDtypeStruct((bsz, s, FOX_WIDTH), BF16),
        jax.ShapeDtypeStruct((bsz, FL_ROWS, s), F32),
    ]
    out_specs = [tok(MLA_HEADS * LANES), tok(MLA_HEADS * LANES), tok(MLA_HEADS * V_DIM),
                 tok(FOX_WIDTH), tok(FOX_WIDTH), tok(FOX_WIDTH),
                 pl.BlockSpec((1, FL_ROWS, ts), lambda b, i: (b, 0, i))]
    return pl.pallas_call(
        _pre_attn_kernel,
        grid=grid,
        in_specs=[tok(d), per_b, per_b, full(g_pre), full(w1), full(wfl), full(bfg), full(g_q),
                  full(wqa), full(wqb), full(g_kv), full(wuk), full(wuv), tok(LANES), tok(LANES)],
        out_specs=out_specs,
        out_shape=outs,
        scratch_shapes=[pltpu.VMEM((FL_ROWS, LANES), F32)],
        compiler_params=_cparams(("arbitrary", "arbitrary")),
        name="pre_attn",
    )(x, sc, sh, g_pre, w1, wfl, bfg, g_q, wqa, wqb, g_kv, wuk, wuv, cp, sp)


def _attn_kernel(*refs, tq, head_lanes, has_bias):
    if has_bias:
        q_ref, k_ref, v_ref, f_ref, o_ref, acc_ref, m_ref, l_ref = refs
    else:
        q_ref, k_ref, v_ref, o_ref, acc_ref, m_ref, l_ref = refs
        f_ref = None
    qi = pl.program_id(2)
    lane = lax.broadcasted_iota(jnp.int32, (tq, LANES), 1)
    if head_lanes == LANES:
        qs = [q_ref[0, :, 0:LANES], q_ref[0, :, LANES:2 * LANES]]
    else:
        q2 = q_ref[0]
        zero = jnp.zeros_like(q2)
        qs = [jnp.where(lane < head_lanes, q2, zero), jnp.where(lane >= head_lanes, q2, zero)]

    acc_ref[...] = jnp.zeros_like(acc_ref)
    m_ref[...] = jnp.full(m_ref.shape, NEG_INF, F32)
    l_ref[...] = jnp.zeros_like(l_ref)

    def step(off, width, masked):
        v2 = v_ref[0, pl.ds(off, width), :]
        for hd in range(2):
            if head_lanes == LANES:
                kk = k_ref[0, pl.ds(off, width), hd * LANES:(hd + 1) * LANES]
            else:
                kk = k_ref[0, pl.ds(off, width), :]
            s = _dot_nt(qs[hd], kk)
            if has_bias:
                s = s - f_ref[0, 0, hd:hd + 1, pl.ds(off, width)]
            if masked:
                rr = lax.broadcasted_iota(jnp.int32, (tq, width), 0)
                cc = lax.broadcasted_iota(jnp.int32, (tq, width), 1)
                s = jnp.where(cc <= rr, s, NEG_INF)
            m = m_ref[hd]
            m_new = jnp.maximum(m, jnp.max(s, axis=-1, keepdims=True))
            alpha = jnp.exp2(m - m_new)
            p = jnp.exp2(s - jnp.tile(m_new, (1, width // LANES)))
            l_ref[hd] = alpha * l_ref[hd] + jnp.sum(p, axis=-1, keepdims=True)
            m_ref[hd] = m_new
            acc_ref[hd] = alpha * acc_ref[hd] + _dot(p.astype(BF16), v2)

    def wide(j, _):
        step(pl.multiple_of(j * (2 * tq), 2 * tq), 2 * tq, False)
        return 0

    lax.fori_loop(0, qi // 2, wide, 0)

    @pl.when(qi % 2 == 1)
    def _():
        step(pl.multiple_of((qi - 1) * tq, tq), tq, False)

    step(pl.multiple_of(qi * tq, tq), tq, True)
    o0 = acc_ref[0] / l_ref[0]
    o1 = acc_ref[1] / l_ref[1]
    o_ref[0] = jnp.where(lane < V_DIM, o0, o1).astype(o_ref.dtype)


def _attention(q, k, v, fcum, tq, head_lanes):
    bsz, s, _ = q.shape
    n_pairs = v.shape[2] // LANES
    qk_w = 2 * head_lanes
    has_bias = fcum is not None
    in_specs = [
        pl.BlockSpec((1, tq, qk_w), lambda b, p, i: (b, i, p)),
        pl.BlockSpec((1, s, qk_w), lambda b, p, i: (b, 0, p)),
        pl.BlockSpec((1, s, LANES), lambda b, p, i: (b, 0, p)),
    ]
    args = [q, k, v]
    if has_bias:
        in_specs.append(pl.BlockSpec((1, 1, 2, s), lambda b, p, i: (b, p, 0, 0)))
        args.append(fcum)
    return pl.pallas_call(
        functools.partial(_attn_kernel, tq=tq, head_lanes=head_lanes, has_bias=has_bias),
        grid=(bsz, n_pairs, s // tq),
        in_specs=in_specs,
        out_specs=pl.BlockSpec((1, tq, LANES), lambda b, p, i: (b, i, p)),
        out_shape=jax.ShapeDtypeStruct((bsz, s, n_pairs * LANES), BF16),
        scratch_shapes=[pltpu.VMEM((2, tq, LANES), F32), pltpu.VMEM((2, tq, LANES), F32),
                        pltpu.VMEM((2, tq, LANES), F32)],
        compiler_params=_cparams(("arbitrary", "arbitrary", "arbitrary")),
        name="attn_fox" if has_bias else "attn_mla",
    )(*args)


def _pack_bf16_pairs(xb):
    w = xb.shape[1] // 2
    lo = lax.bitcast_convert_type(xb[:, :w].astype(F32), jnp.uint32)
    hi = lax.bitcast_convert_type(xb[:, w:].astype(F32), jnp.uint32)
    return (hi & jnp.uint32(0xFFFF0000)) | lax.shift_right_logical(lo, jnp.uint32(16))


def _unpack_bf16_pairs(xw):
    lo = lax.bitcast_convert_type(lax.shift_left(xw, jnp.uint32(16)), F32)
    hi = lax.bitcast_convert_type(xw & jnp.uint32(0xFFFF0000), F32)
    return jnp.concatenate([lo, hi], axis=1).astype(BF16)


def _post_attn_kernel(om_ref, of_ref, x_ref, wo_ref, gpost_ref, gate_ref, gmoe_ref, sc_ref, sh_ref,
                      wr_ref, br_ref, x1_ref, h2_ref, idx_ref, gts_ref):
    half = om_ref.shape[2]
    o = _dot(om_ref[0], wo_ref[0:half, :]) + _dot(of_ref[0], wo_ref[half:, :])
    x1 = x_ref[0] + gate_ref[0] * _rms(o, gpost_ref[...])
    x1_ref[0] = x1
    h2 = _rms(x1, gmoe_ref[...]) * (1.0 + sc_ref[0]) + sh_ref[0]
    hb = h2.astype(BF16)
    h2_ref[0] = _pack_bf16_pairs(hb)
    logits = _dot_nt(wr_ref[...], hb) + br_ref[...]
    n_e = logits.shape[0]
    eid = lax.broadcasted_iota(jnp.int32, logits.shape, 0)
    vals, idxs = [], []
    for _ in range(TOP_K):
        m = jnp.max(logits, axis=0, keepdims=True)
        ix = jnp.min(jnp.where(logits == m, eid, n_e), axis=0, keepdims=True)
        vals.append(m)
        idxs.append(ix)
        logits = jnp.where(eid == ix, -jnp.inf, logits)
    es = [jnp.exp(vv - vals[0]) for vv in vals]
    den = es[0] + es[1] + es[2] + es[3]
    for kk in range(TOP_K):
        idx_ref[0, kk:kk + 1, :] = idxs[kk]
        gts_ref[0, kk:kk + 1, :] = es[kk] / den


def _post_attn(o_mla, o_fox, x, wo, g_post, gate, g_moe, sc, sh, wr_t, br, ts):
    bsz, s, d = x.shape
    half = o_mla.shape[2]
    tok = lambda width: pl.BlockSpec((1, ts, width), lambda b, i: (b, i, 0))
    per_b = pl.BlockSpec((1, 1, d), lambda b, i: (b, 0, 0))
    full = lambda a: pl.BlockSpec(a.shape, lambda b, i: (0,) * a.ndim)
    k_spec = pl.BlockSpec((1, TOP_K, ts), lambda b, i: (b, 0, i))
    return pl.pallas_call(
        _post_attn_kernel,
        grid=(bsz, s // ts),
        in_specs=[tok(half), tok(half), tok(d), full(wo), full(g_post), per_b, full(g_moe), per_b,
                  per_b, full(wr_t), full(br)],
        out_specs=[tok(d), tok(d // 2), k_spec, k_spec],
        out_shape=[jax.ShapeDtypeStruct((bsz, s, d), F32),
                   jax.ShapeDtypeStruct((bsz, s, d // 2), jnp.uint32),
                   jax.ShapeDtypeStruct((bsz, TOP_K, s), jnp.int32),
                   jax.ShapeDtypeStruct((bsz, TOP_K, s), F32)],
        compiler_params=_cparams(("arbitrary", "arbitrary")),
        name="post_attn",
    )(o_mla, o_fox, x, wo, g_post, gate, g_moe, sc, sh, wr_t, br)


def _route_kernel(idx_ref, tri_ref, rank_ref, cnt_ref, run_ref):
    first = jnp.logical_and(pl.program_id(0) == 0, pl.program_id(1) == 0)

    @pl.when(first)
    def _():
        run_ref[...] = jnp.zeros_like(run_ref)

    tr = idx_ref.shape[2]
    eid = lax.broadcasted_iota(jnp.int32, (N_EXPERTS, tr), 0)
    hot = [eid == idx_ref[0, kk:kk + 1, :] for kk in range(TOP_K)]
    multi = hot[0] | hot[1] | hot[2] | hot[3]
    mf = jnp.where(multi, 1.0, 0.0)
    incl = _dot(mf.astype(BF16), tri_ref[...]) + run_ref[:, 0:1]
    excl = incl - mf
    for kk in range(TOP_K):
        rank_ref[0, kk:kk + 1, :] = jnp.sum(jnp.where(hot[kk], excl, 0.0), axis=0,
                                            keepdims=True).astype(jnp.int32)
    run_ref[...] = jnp.broadcast_to(incl[:, tr - 1:tr], run_ref.shape)
    cnt_ref[...] = run_ref[...].astype(jnp.int32)


def _route(idx_kt, tr):
    bsz, _, s = idx_kt.shape
    tri = (jnp.arange(tr)[:, None] <= jnp.arange(tr)[None, :]).astype(BF16)
    k_spec = pl.BlockSpec((1, TOP_K, tr), lambda b, i: (b, 0, i))
    return pl.pallas_call(
        _route_kernel,
        grid=(bsz, s // tr),
        in_specs=[k_spec, pl.BlockSpec((tr, tr), lambda b, i: (0, 0))],
        out_specs=[k_spec, pl.BlockSpec((N_EXPERTS, LANES), lambda b, i: (0, 0))],
        out_shape=[jax.ShapeDtypeStruct((bsz, TOP_K, s), jnp.int32),
                   jax.ShapeDtypeStruct((N_EXPERTS, LANES), jnp.int32)],
        scratch_shapes=[pltpu.VMEM((N_EXPERTS, LANES), F32)],
        compiler_params=_cparams(("arbitrary", "arbitrary")),
        name="route",
    )(idx_kt, tri)


SC_CORES = 2
SC_SUBCORES = 16
SC_CHUNK = 64
MOE_GROUPS = 4


def _sc_gather(table, idx):
    m = idx.shape[0]
    width = table.shape[1]
    n_workers = SC_CORES * SC_SUBCORES
    per_worker = m // n_workers
    n_chunks = per_worker // SC_CHUNK
    assert n_chunks * SC_CHUNK * n_workers == m
    n_buf = next(nb for nb in (3, 2, 1) if n_chunks % nb == 0)
    mesh = plsc.VectorSubcoreMesh(core_axis_name="c", subcore_axis_name="s")

    @functools.partial(
        pl.kernel, mesh=mesh,
        out_type=jax.ShapeDtypeStruct((m, width), table.dtype),
        scratch_types=[pltpu.VMEM((n_chunks, SC_CHUNK), jnp.int32),
                       pltpu.VMEM((n_buf, SC_CHUNK, width), table.dtype),
                       pltpu.SemaphoreType.DMA((n_buf,)),
                       pltpu.SemaphoreType.DMA((n_buf,))],
        name="sc_gather",
    )
    def gather(table_hbm, idx_hbm, out_hbm, idx_v, rows_v, sem_g, sem_w):
        wid = lax.axis_index("s") * SC_CORES + lax.axis_index("c")
        base = wid * per_worker
        pltpu.sync_copy(idx_hbm.at[wid], idx_v)

        def gather_copy(j, b):
            return pltpu.make_async_copy(table_hbm.at[idx_v.at[j]], rows_v.at[b], sem_g.at[b])

        for b in range(n_buf):
            gather_copy(b, b).start()

        @pl.loop(0, n_chunks, step=n_buf)
        def _(g):
            for b in range(n_buf):
                j = g + b
                gather_copy(j, b).wait()
                off = pl.multiple_of(base + j * SC_CHUNK, SC_CHUNK)
                write = pltpu.make_async_copy(rows_v.at[b], out_hbm.at[pl.ds(off, SC_CHUNK)],
                                              sem_w.at[b])
                write.start()
                write.wait()

                @pl.when(j + n_buf < n_chunks)
                def _():
                    gather_copy(j + n_buf, b).start()

    return gather(table, idx.reshape(n_workers, n_chunks, SC_CHUNK))


def _expert_mlp(xb, wgu_bf, bgu_ref, wd_bf, bd_ref):
    ff = wd_bf.shape[0]
    gu = _dot(xb, wgu_bf[...]) + bgu_ref[0]
    g = jnp.minimum(gu[:, :ff], SWIGLU_LIMIT)
    u = jnp.clip(gu[:, ff:], -SWIGLU_LIMIT, SWIGLU_LIMIT)
    glu = g * jax.nn.sigmoid(SWIGLU_ALPHA * g)
    act = ((u + 1.0) * glu).astype(BF16)
    return _dot(act, wd_bf[...]) + bd_ref[0]


def _expert_pregathered_kernel(be_ref, first_ref, n_ref, sidx_ref, x_ref, wgu_ref, bgu_ref, wd_ref,
                               bd_ref, *rest):
    y_hbm, ybuf0, ybuf1, wgu_bf, wd_bf, sem_s = rest[-6:]
    i = pl.program_id(0)
    n = n_ref[0]
    bm = ybuf0.shape[0]
    ybufs = (ybuf0, ybuf1)

    def scatter_start(slot, r, priority=0):
        pltpu.make_async_copy(ybufs[slot].at[pl.ds(r, 1), :],
                              y_hbm.at[pl.ds(sidx_ref[0, 0, r], 1), :],
                              sem_s.at[slot]).start(priority)

    def scatter_wait(slot):
        pltpu.make_async_copy(ybufs[slot], y_hbm.at[pl.ds(0, bm), :], sem_s.at[slot]).wait()

    @pl.when(jnp.logical_and(first_ref[i] == 1, i < n))
    def _():
        wgu_bf[...] = wgu_ref[0].astype(BF16)
        wd_bf[...] = wd_ref[0].astype(BF16)

    def step(par):
        if par == 0:
            @pl.when(i == 0)
            def _():
                ybuf1[...] = jnp.zeros_like(ybuf1)

        @pl.when(jnp.logical_and(i >= 1, i <= n))
        def _():
            scatter_wait(par)

        @pl.when(i < n)
        def _():
            xb = _unpack_bf16_pairs(x_ref[...])
            for r in range(bm):
                scatter_start(1 - par, r, r % 2)
            ybufs[par][...] = _expert_mlp(xb, wgu_bf, bgu_ref, wd_bf, bd_ref)

        @pl.when(i == n)
        def _():
            def body(r, _):
                scatter_start(1 - par, r)
                return 0
            lax.fori_loop(0, bm, body, 0, unroll=8)
            scatter_wait(1 - par)

    @pl.when(i % 2 == 0)
    def _():
        step(0)

    @pl.when(i % 2 == 1)
    def _():
        step(1)


def _experts_pregathered(be_step, first_step, n_real, sidx, xs, w_gu, b_gu, w_d, b_d, y4, n_out_rows,
                         bm):
    e, d, f2 = w_gu.shape
    ff = w_d.shape[1]
    n_steps = sidx.shape[0]
    n_xblocks = xs.shape[0] // bm
    n_prefetch = 3
    prev = [] if y4 is None else [y4]
    return pl.pallas_call(
        _expert_pregathered_kernel,
        grid_spec=pltpu.PrefetchScalarGridSpec(
            num_scalar_prefetch=n_prefetch,
            grid=(n_steps,),
            in_specs=[
                pl.BlockSpec((1, 1, bm), lambda i, be, fi, nu: (i, 0, 0), memory_space=pltpu.SMEM),
                pl.BlockSpec((bm, d // 2), lambda i, be, fi, nu: (jnp.minimum(i, n_xblocks - 1), 0)),
                pl.BlockSpec((1, d, f2), lambda i, be, fi, nu: (be[i], 0, 0)),
                pl.BlockSpec((1, 1, f2), lambda i, be, fi, nu: (be[i], 0, 0)),
                pl.BlockSpec((1, ff, d), lambda i, be, fi, nu: (be[i], 0, 0)),
                pl.BlockSpec((1, 1, d), lambda i, be, fi, nu: (be[i], 0, 0)),
            ] + [pl.BlockSpec(memory_space=pl.ANY) for _ in prev],
            out_specs=pl.BlockSpec(memory_space=pl.ANY),
            scratch_shapes=[pltpu.VMEM((bm, d), F32), pltpu.VMEM((bm, d), F32),
                            pltpu.VMEM((d, f2), BF16), pltpu.VMEM((ff, d), BF16),
                            pltpu.SemaphoreType.DMA((2,))],
        ),
        out_shape=jax.ShapeDtypeStruct((n_out_rows, d), F32),
        input_output_aliases={n_prefetch + 6: 0} if prev else {},
        compiler_params=_cparams(("arbitrary",)),
        name="experts",
    )(be_step, first_step, n_real, sidx, xs, w_gu, b_gu.reshape(e, 1, f2), w_d, b_d.reshape(e, 1, d),
      *prev)


def _combine_kernel(y0_ref, y1_ref, y2_ref, y3_ref, gts_ref, x1_ref, gpost_ref, gate_ref, o_ref):
    gts = gts_ref[...]
    acc = gts[:, 0:1] * y0_ref[...]
    for kk, y_ref in enumerate((y1_ref, y2_ref, y3_ref), start=1):
        acc = acc + gts[:, kk:kk + 1] * y_ref[...]
    o_ref[0] = x1_ref[0] + gate_ref[0] * _rms(acc, gpost_ref[...])


def _combine(y4, gts, x1, g_post, gate, tt):
    bsz, s, d = x1.shape
    n_t = s // tt
    n_tiles = bsz * n_t
    y_spec = lambda kk: pl.BlockSpec((tt, d), lambda b, i: (kk * n_tiles + b * n_t + i, 0))
    return pl.pallas_call(
        _combine_kernel,
        grid=(bsz, n_t),
        in_specs=[
            y_spec(0), y_spec(1), y_spec(2), y_spec(3),
            pl.BlockSpec((tt, TOP_K), lambda b, i: (b * n_t + i, 0)),
            pl.BlockSpec((1, tt, d), lambda b, i: (b, i, 0)),
            pl.BlockSpec(g_post.shape, lambda b, i: (0, 0)),
            pl.BlockSpec((1, 1, d), lambda b, i: (b, 0, 0)),
        ],
        out_specs=pl.BlockSpec((1, tt, d), lambda b, i: (b, i, 0)),
        out_shape=jax.ShapeDtypeStruct((bsz, s, d), F32),
        compiler_params=_cparams(("arbitrary", "arbitrary")),
        name="combine",
    )(y4, y4, y4, y4, gts, x1, g_post, gate)


def _rope_patterns(positions):
    inv_freq = ROPE_THETA ** (-jnp.arange(0, ROPE_DIM, 2, dtype=F32) / ROPE_DIM)
    ang = positions.astype(F32)[..., None] * inv_freq
    cos, sin = jnp.cos(ang), jnp.sin(ang)
    ones = jnp.ones(positions.shape + (NOPE_DIM,), F32)
    zeros = jnp.zeros(positions.shape + (LANES - NOPE_DIM - ROPE_DIM,), F32)
    cp = jnp.concatenate([ones, cos, cos, zeros], axis=-1)
    sp = jnp.concatenate([0.0 * ones, -sin, sin, zeros], axis=-1)
    return cp, sp


def _prep_mixer_weights(w_in, w_uq, w_ukv, b_forget):
    d = w_in.shape[0]
    o_kr = Q_RANK + KV_RANK
    o_f = o_kr + ROPE_DIM
    hr = ROPE_DIM // 2
    z = lambda n: jnp.zeros((d, n), w_in.dtype)
    kr = w_in[:, o_kr:o_f]
    kr_sw = jnp.concatenate([kr[:, hr:], kr[:, :hr]], axis=1)
    pad = LANES - NOPE_DIM - ROPE_DIM
    w1 = jnp.concatenate([
        w_in[:, :o_kr],
        z(NOPE_DIM), kr, z(pad),
        z(NOPE_DIM), kr_sw, z(pad),
        w_in[:, o_f:o_f + 3 * FOX_WIDTH],
    ], axis=1).astype(BF16)
    wfl = jnp.zeros((FL_ROWS, d), w_in.dtype).at[:FOX_HEADS].set(
        w_in[:, o_f + 3 * FOX_WIDTH:].T).astype(BF16)
    bfg = jnp.zeros((FL_ROWS, 1), F32).at[:FOX_HEADS, 0].set(b_forget)

    wq = w_uq.reshape(Q_RANK, MLA_HEADS, NOPE_DIM + ROPE_DIM)
    nope, rope = wq[..., :NOPE_DIM], wq[..., NOPE_DIM:]
    zq = lambda n: jnp.zeros((Q_RANK, MLA_HEADS, n), w_uq.dtype)
    wqa = jnp.concatenate([nope, rope, zq(pad)], axis=-1).reshape(Q_RANK, -1).astype(BF16)
    wqb = jnp.concatenate([zq(NOPE_DIM), rope[..., hr:], rope[..., :hr], zq(pad)],
                          axis=-1).reshape(Q_RANK, -1).astype(BF16)
    wkv = w_ukv.reshape(KV_RANK, MLA_HEADS, NOPE_DIM + V_DIM)
    wuk = jnp.concatenate([wkv[..., :NOPE_DIM],
                           jnp.zeros((KV_RANK, MLA_HEADS, LANES - NOPE_DIM), w_ukv.dtype)],
                          axis=-1).reshape(KV_RANK, -1).astype(BF16)
    wuv = wkv[..., NOPE_DIM:].reshape(KV_RANK, -1).astype(BF16)
    return w1, wfl, bfg, wqa, wqb, wuk, wuv


def _slot_tables(idx_kt, rank_kt, counts, bm):
    bsz, _, s = idx_kt.shape
    t = bsz * s
    n_blocks = (t * TOP_K) // bm + N_EXPERTS
    n_real = t * TOP_K
    i32 = jnp.int32
    eids = jnp.arange(N_EXPERTS, dtype=i32)

    pcounts = ((counts + bm - 1) // bm) * bm
    incl_mat = (eids[:, None] <= eids[None, :]).astype(i32)
    pends = pcounts @ incl_mat
    pstarts = pends - pcounts
    cstarts = counts @ incl_mat - counts
    total = pends[-1]
    n_used = (total // bm).astype(i32).reshape(1)

    onehot = idx_kt[..., None] == eids
    dest = jnp.sum(jnp.where(onehot, pstarts, 0), axis=-1) + rank_kt
    tok = jnp.arange(t, dtype=i32).reshape(bsz, 1, s)
    kk = jnp.arange(TOP_K, dtype=i32).reshape(1, TOP_K, 1)
    yrow_real = jnp.broadcast_to(kk * t + tok, dest.shape)
    _, rows_sorted = lax.sort((dest.reshape(-1), yrow_real.reshape(-1)), num_keys=1)

    blk0 = jnp.arange(n_blocks, dtype=i32)
    be_blk = jnp.minimum(jnp.sum((pends[None, :] <= (blk0 * bm)[:, None]).astype(i32), axis=1),
                         N_EXPERTS - 1)
    hot_b = be_blk[:, None] == eids[None, :]
    pick = lambda v: jnp.sum(jnp.where(hot_b, v[None, :], 0), axis=1)[:, None]
    local = (blk0 * bm)[:, None] + jnp.arange(bm, dtype=i32)[None, :] - pick(pstarts)
    valid = jnp.logical_and(local < pick(counts), (blk0 < n_used[0])[:, None])
    j = jnp.clip(pick(cstarts) + local, 0, n_real - 1)
    rows = jnp.take(rows_sorted, j.reshape(-1), axis=0).reshape(n_blocks, bm)
    spare = TOP_K * t + jnp.arange(bm, dtype=i32)[None, :]
    yrow = jnp.where(valid, rows, spare)
    src_tok = jnp.where(valid, rows % t, 0)

    return src_tok, yrow, be_blk, n_used[0]


def _first_flags(be_step):
    return jnp.concatenate([jnp.ones((1,), jnp.int32),
                            (be_step[1:] != be_step[:-1]).astype(jnp.int32)])


def _moe_blocks(src_tok, yrow, be_blk, n_used, h2p, w_gu, b_gu, w_d, b_d, n_out_rows, bm, n_groups):
    i32 = jnp.int32
    n_blocks = src_tok.shape[0]
    nb = n_blocks // n_groups
    spare = (n_out_rows - bm) + jnp.arange(bm, dtype=i32)[None, :]
    y4 = None
    for g in range(n_groups):
        lo = g * nb
        xs = _sc_gather(h2p, src_tok[lo:lo + nb].reshape(-1))
        sidx = jnp.concatenate([spare, yrow[lo:lo + nb]]).reshape(nb + 1, 1, bm)
        be = be_blk[lo + jnp.minimum(jnp.arange(nb + 1, dtype=i32), nb - 1)]
        n_g = jnp.clip(n_used - lo, 0, nb).astype(i32).reshape(1)
        y4 = _experts_pregathered(be, _first_flags(be), n_g, sidx, xs, w_gu, b_gu, w_d, b_d, y4,
                                  n_out_rows, bm)
    return y4


def _layer(x, ada, cp, sp, g_attn_pre, g_attn_post, w_in, g_q_norm, w_uq, g_kv_norm, w_ukv,
           b_forget, w_out, g_moe_pre, g_moe_post, w_router, b_router, w_gate_up, b_gate_up,
           w_down, b_down):
    bsz, s, d = x.shape
    t = bsz * s
    ts = min(512, s)
    tq = min(512, s)
    tt = min(512, s)
    tr = min(1024, s)
    bm = 512
    row = lambda v: v.reshape(1, -1)
    sh_a, sc_a, gt_a, sh_m, sc_m, gt_m = [v.reshape(bsz, 1, d) for v in jnp.split(ada, 6, axis=-1)]

    w1, wfl, bfg, wqa, wqb, wuk, wuv = _prep_mixer_weights(w_in, w_uq, w_ukv, b_forget)
    q, k, v, fq, fk, fv, fcum = _pre_attn(x, sc_a, sh_a, row(g_attn_pre), w1, wfl, bfg,
                                          row(g_q_norm), wqa, wqb, row(g_kv_norm), wuk, wuv,
                                          cp, sp, ts)
    o_mla = _attention(q, k, v, None, tq, LANES)
    fpairs = fcum[:, :FOX_HEADS].reshape(bsz, FOX_HEADS // 2, 2, s)
    o_fox = _attention(fq, fk, fv, fpairs, tq, FOX_DIM)

    x1, h2, idx_kt, gts_kt = _post_attn(o_mla, o_fox, x, w_out.astype(BF16), row(g_attn_post), gt_a,
                                        row(g_moe_pre), sc_m, sh_m, w_router.T.astype(BF16),
                                        b_router.reshape(-1, 1), ts)

    rank_kt, cnt = _route(idx_kt, tr)
    src_tok, yrow, be_blk, n_used = _slot_tables(idx_kt, rank_kt, cnt[:, 0], bm)
    y4 = _moe_blocks(src_tok, yrow, be_blk, n_used, h2.reshape(t, d // 2), w_gate_up, b_gate_up,
                     w_down, b_down, TOP_K * t + bm, bm, MOE_GROUPS)
    gts = gts_kt.transpose(0, 2, 1).reshape(t, TOP_K)
    return _combine(y4, gts, x1, row(g_moe_post), gt_m, tt)


def kernel(x, c, positions, w_ada, b_ada, g_attn_pre, g_attn_post, w_in, g_q_norm, w_uq, g_kv_norm,
           w_ukv, b_forget, w_out, g_moe_pre, g_moe_post, w_router, b_router, w_gate_up, b_gate_up,
           w_down, b_down):
    cp, sp = _rope_patterns(positions)
    for layer in range(w_ada.shape[0]):
        ada = _ada(c, w_ada[layer], b_ada[layer])
        x = _layer(x, ada, cp, sp, g_attn_pre[layer], g_attn_post[layer], w_in[layer],
                   g_q_norm[layer], w_uq[layer], g_kv_norm[layer], w_ukv[layer], b_forget[layer],
                   w_out[layer], g_moe_pre[layer], g_moe_post[layer], w_router[layer],
                   b_router[layer], w_gate_up[layer], b_gate_up[layer], w_down[layer],
                   b_down[layer])
    return x
```

```python
import functools
import math

import jax
import jax.numpy as jnp
from jax import lax
from jax.experimental import pallas as pl
from jax.experimental.pallas import tpu as pltpu
from jax.experimental.pallas import tpu_sc as plsc

F32 = jnp.float32
BF16 = jnp.bfloat16

MLA_HEADS = 8
NOPE_DIM = 64
ROPE_DIM = 32
V_DIM = 64
Q_RANK = 256
KV_RANK = 128
FOX_HEADS = 8
FOX_DIM = 64
FOX_WIDTH = FOX_HEADS * FOX_DIM
ROPE_THETA = 10000.0
N_EXPERTS = 32
TOP_K = 4
SWIGLU_ALPHA = 1.702
SWIGLU_LIMIT = 7.0
RMS_EPS = 1e-6
NEG_INF = -1e30
LOG2E = math.log2(math.e)
MLA_SCALE = LOG2E / math.sqrt(NOPE_DIM + ROPE_DIM)
FOX_SCALE = LOG2E / math.sqrt(FOX_DIM)

LANES = 128
C_CQ = (0, 256)
C_CKV = (256, 384)
C_KR = (384, 640)
C_FQ = (640, 1152)
C_FK = (1152, 1664)
C_FV = (1664, 2176)
W1_COLS = 2176
FL_ROWS = 16

VMEM_LIMIT = 56 * 1024 * 1024


def _cparams(sem):
    return pltpu.CompilerParams(dimension_semantics=sem, vmem_limit_bytes=VMEM_LIMIT)


def _rms(x, g):
    return x * lax.rsqrt(jnp.mean(x * x, axis=-1, keepdims=True) + RMS_EPS) * g


def _dot(a, b):
    return jnp.dot(a, b, preferred_element_type=F32)


def _dot_nt(a, b):
    return lax.dot_general(a, b, (((1,), (1,)), ((), ())), preferred_element_type=F32)


def _ada_kernel(c_ref, w_ref, b_ref, o_ref):
    c = c_ref[...]
    ca = (c * jax.nn.sigmoid(c)).astype(BF16)
    o_ref[...] = _dot(ca, w_ref[...].astype(BF16)) + b_ref[...]


def _ada(c, w, b):
    bsz, d = c.shape
    n = w.shape[1]
    bn = 1024
    return pl.pallas_call(
        _ada_kernel,
        grid=(n // bn,),
        in_specs=[
            pl.BlockSpec((bsz, d), lambda j: (0, 0)),
            pl.BlockSpec((d, bn), lambda j: (0, j)),
            pl.BlockSpec((1, bn), lambda j: (0, j)),
        ],
        out_specs=pl.BlockSpec((bsz, bn), lambda j: (0, j)),
        out_shape=jax.ShapeDtypeStruct((bsz, n), F32),
        compiler_params=_cparams(("arbitrary",)),
        name="ada",
    )(c, w, b.reshape(1, n))


def _pre_attn_kernel(x_ref, sc_ref, sh_ref, gpre_ref, w1_ref, wfl_ref, bf_ref, gq_ref, wqa_ref,
                     wqb_ref, gkv_ref, wuk_ref, wuv_ref, cp_ref, sp_ref,
                     q_ref, k_ref, v_ref, fq_ref, fk_ref, fv_ref, fc_ref, carry_ref):
    si = pl.program_id(1)
    ts = x_ref.shape[1]
    x = x_ref[0]
    h = _rms(x, gpre_ref[...]) * (1.0 + sc_ref[0]) + sh_ref[0]
    hb = h.astype(BF16)

    def proj(c):
        return _dot(hb, w1_ref[:, c[0]:c[1]])

    cp = cp_ref[0]
    sp = sp_ref[0]

    cqn = _rms(proj(C_CQ), gq_ref[...]).astype(BF16)
    qa = _dot(cqn, wqa_ref[...])
    qb = _dot(cqn, wqb_ref[...])
    for hd in range(MLA_HEADS):
        sl = slice(hd * LANES, (hd + 1) * LANES)
        q_ref[0, :, sl] = ((qa[:, sl] * cp + qb[:, sl] * sp) * MLA_SCALE).astype(BF16)

    kr = proj(C_KR)
    k_rope = kr[:, :LANES] * cp + kr[:, LANES:] * sp
    ckvn = _rms(proj(C_CKV), gkv_ref[...]).astype(BF16)
    kn = _dot(ckvn, wuk_ref[...])
    for hd in range(MLA_HEADS):
        sl = slice(hd * LANES, (hd + 1) * LANES)
        k_ref[0, :, sl] = (kn[:, sl] + k_rope).astype(BF16)
    v_ref[0] = _dot(ckvn, wuv_ref[...]).astype(BF16)

    fq_ref[0] = (proj(C_FQ) * FOX_SCALE).astype(BF16)
    fk_ref[0] = proj(C_FK).astype(BF16)
    fv_ref[0] = proj(C_FV).astype(BF16)

    fl = _dot_nt(wfl_ref[...], hb) + bf_ref[...]
    lf = jnp.minimum(fl, 0.0) - jnp.log1p(jnp.exp(-jnp.abs(fl)))
    r = lax.broadcasted_iota(jnp.int32, (ts, ts), 0)
    c = lax.broadcasted_iota(jnp.int32, (ts, ts), 1)
    tri = (r <= c).astype(BF16)
    p0 = lf.astype(BF16)
    r1 = lf - p0.astype(F32)
    p1 = r1.astype(BF16)
    p2 = (r1 - p1.astype(F32)).astype(BF16)
    cs = _dot(p0, tri) + _dot(p1, tri) + _dot(p2, tri)

    @pl.when(si == 0)
    def _():
        carry_ref[...] = jnp.zeros_like(carry_ref)

    cs = cs + carry_ref[:, 0:1]
    fc_ref[0] = cs * LOG2E
    carry_ref[...] = jnp.broadcast_to(cs[:, ts - 1:ts], carry_ref.shape)


def _pre_attn(x, sc, sh, g_pre, w1, wfl, bfg, g_q, wqa, wqb, g_kv, wuk, wuv, cp, sp, ts):
    bsz, s, d = x.shape
    grid = (bsz, s // ts)
    tok = lambda width: pl.BlockSpec((1, ts, width), lambda b, i: (b, i, 0))
    per_b = pl.BlockSpec((1, 1, d), lambda b, i: (b, 0, 0))
    full = lambda a: pl.BlockSpec(a.shape, lambda b, i: (0,) * a.ndim)
    outs = [
        jax.ShapeDtypeStruct((bsz, s, MLA_HEADS * LANES), BF16),
        jax.ShapeDtypeStruct((bsz, s, MLA_HEADS * LANES), BF16),
        jax.ShapeDtypeStruct((bsz, s, MLA_HEADS * V_DIM), BF16),
        jax.ShapeDtypeStruct((bsz, s, FOX_WIDTH), BF16),
        jax.ShapeDtypeStruct((bsz, s, FOX_WIDTH), BF16),
        jax.ShapeDtypeStruct((bsz, s, FOX_WIDTH), BF16),
        jax.ShapeDtypeStruct((bsz, FL_ROWS, s), F32),
    ]
    out_specs = [tok(MLA_HEADS * LANES), tok(MLA_HEADS * LANES), tok(MLA_HEADS * V_DIM),
                 tok(FOX_WIDTH), tok(FOX_WIDTH), tok(FOX_WIDTH),
                 pl.BlockSpec((1, FL_ROWS, ts), lambda b, i: (b, 0, i))]
    return pl.pallas_call(
        _pre_attn_kernel,
        grid=grid,
        in_specs=[tok(d), per_b, per_b, full(g_pre), full(w1), full(wfl), full(bfg), full(g_q),
                  full(wqa), full(wqb), full(g_kv), full(wuk), full(wuv), tok(LANES), tok(LANES)],
        out_specs=out_specs,
        out_shape=outs,
        scratch_shapes=[pltpu.VMEM((FL_ROWS, LANES), F32)],
        compiler_params=_cparams(("arbitrary", "arbitrary")),
        name="pre_attn",
    )(x, sc, sh, g_pre, w1, wfl, bfg, g_q, wqa, wqb, g_kv, wuk, wuv, cp, sp)


def _attn_kernel(*refs, tq, head_lanes, has_bias):
    if has_bias:
        q_ref, k_ref, v_ref, f_ref, o_ref, acc_ref, m_ref, l_ref = refs
    else:
        q_ref, k_ref, v_ref, o_ref, acc_ref, m_ref, l_ref = refs
        f_ref = None
    qi = pl.program_id(2)
    lane = lax.broadcasted_iota(jnp.int32, (tq, LANES), 1)
    if head_lanes == LANES:
        qs = [q_ref[0, :, 0:LANES], q_ref[0, :, LANES:2 * LANES]]
    else:
        q2 = q_ref[0]
        zero = jnp.zeros_like(q2)
        qs = [jnp.where(lane < head_lanes, q2, zero), jnp.where(lane >= head_lanes, q2, zero)]

    acc_ref[...] = jnp.zeros_like(acc_ref)
    m_ref[...] = jnp.full(m_ref.shape, NEG_INF, F32)
    l_ref[...] = jnp.zeros_like(l_ref)

    def step(off, width, masked):
        v2 = v_ref[0, pl.ds(off, width), :]
        for hd in range(2):
            if head_lanes == LANES:
                kk = k_ref[0, pl.ds(off, width), hd * LANES:(hd + 1) * LANES]
            else:
                kk = k_ref[0, pl.ds(off, width), :]
            s = _dot_nt(qs[hd], kk)
            if has_bias:
                s = s - f_ref[0, 0, hd:hd + 1, pl.ds(off, width)]
            if masked:
                rr = lax.broadcasted_iota(jnp.int32, (tq, width), 0)
                cc = lax.broadcasted_iota(jnp.int32, (tq, width), 1)
                s = jnp.where(cc <= rr, s, NEG_INF)
            m = m_ref[hd]
            m_new = jnp.maximum(m, jnp.max(s, axis=-1, keepdims=True))
            alpha = jnp.exp2(m - m_new)
            p = jnp.exp2(s - jnp.tile(m_new, (1, width // LANES)))
            l_ref[hd] = alpha * l_ref[hd] + jnp.sum(p, axis=-1, keepdims=True)
            m_ref[hd] = m_new
            acc_ref[hd] = alpha * acc_ref[hd] + _dot(p.astype(BF16), v2)

    def wide(j, _):
        step(pl.multiple_of(j * (2 * tq), 2 * tq), 2 * tq, False)
        return 0

    lax.fori_loop(0, qi // 2, wide, 0)

    @pl.when(qi % 2 == 1)
    def _():
        step(pl.multiple_of((qi - 1) * tq, tq), tq, False)

    step(pl.multiple_of(qi * tq, tq), tq, True)
    o0 = acc_ref[0] / l_ref[0]
    o1 = acc_ref[1] / l_ref[1]
    o_ref[0] = jnp.where(lane < V_DIM, o0, o1).astype(o_ref.dtype)


def _attention(q, k, v, fcum, tq, head_lanes):
    bsz, s, _ = q.shape
    n_pairs = v.shape[2] // LANES
    qk_w = 2 * head_lanes
    has_bias = fcum is not None
    in_specs = [
        pl.BlockSpec((1, tq, qk_w), lambda b, p, i: (b, i, p)),
        pl.BlockSpec((1, s, qk_w), lambda b, p, i: (b, 0, p)),
        pl.BlockSpec((1, s, LANES), lambda b, p, i: (b, 0, p)),
    ]
    args = [q, k, v]
    if has_bias:
        in_specs.append(pl.BlockSpec((1, 1, 2, s), lambda b, p, i: (b, p, 0, 0)))
        args.append(fcum)
    return pl.pallas_call(
        functools.partial(_attn_kernel, tq=tq, head_lanes=head_lanes, has_bias=has_bias),
        grid=(bsz, n_pairs, s // tq),
        in_specs=in_specs,
        out_specs=pl.BlockSpec((1, tq, LANES), lambda b, p, i: (b, i, p)),
        out_shape=jax.ShapeDtypeStruct((bsz, s, n_pairs * LANES), BF16),
        scratch_shapes=[pltpu.VMEM((2, tq, LANES), F32), pltpu.VMEM((2, tq, LANES), F32),
                        pltpu.VMEM((2, tq, LANES), F32)],
        compiler_params=_cparams(("arbitrary", "arbitrary", "arbitrary")),
        name="attn_fox" if has_bias else "attn_mla",
    )(*args)


def _pack_bf16_pairs(xb):
    w = xb.shape[1] // 2
    lo = lax.bitcast_convert_type(xb[:, :w].astype(F32), jnp.uint32)
    hi = lax.bitcast_convert_type(xb[:, w:].astype(F32), jnp.uint32)
    return (hi & jnp.uint32(0xFFFF0000)) | lax.shift_right_logical(lo, jnp.uint32(16))


def _unpack_bf16_pairs(xw):
    lo = lax.bitcast_convert_type(lax.shift_left(xw, jnp.uint32(16)), F32)
    hi = lax.bitcast_convert_type(xw & jnp.uint32(0xFFFF0000), F32)
    return jnp.concatenate([lo, hi], axis=1).astype(BF16)


def _post_attn_kernel(om_ref, of_ref, x_ref, wo_ref, gpost_ref, gate_ref, gmoe_ref, sc_ref, sh_ref,
                      wr_ref, br_ref, x1_ref, h2_ref, idx_ref, gts_ref):
    half = om_ref.shape[2]
    o = _dot(om_ref[0], wo_ref[0:half, :]) + _dot(of_ref[0], wo_ref[half:, :])
    x1 = x_ref[0] + gate_ref[0] * _rms(o, gpost_ref[...])
    x1_ref[0] = x1
    h2 = _rms(x1, gmoe_ref[...]) * (1.0 + sc_ref[0]) + sh_ref[0]
    hb = h2.astype(BF16)
    h2_ref[0] = _pack_bf16_pairs(hb)
    logits = _dot_nt(wr_ref[...], hb) + br_ref[...]
    n_e = logits.shape[0]
    eid = lax.broadcasted_iota(jnp.int32, logits.shape, 0)
    vals, idxs = [], []
    for _ in range(TOP_K):
        m = jnp.max(logits, axis=0, keepdims=True)
        ix = jnp.min(jnp.where(logits == m, eid, n_e), axis=0, keepdims=True)
        vals.append(m)
        idxs.append(ix)
        logits = jnp.where(eid == ix, -jnp.inf, logits)
    es = [jnp.exp(vv - vals[0]) for vv in vals]
    den = es[0] + es[1] + es[2] + es[3]
    for kk in range(TOP_K):
        idx_ref[0, kk:kk + 1, :] = idxs[kk]
        gts_ref[0, kk:kk + 1, :] = es[kk] / den


def _post_attn(o_mla, o_fox, x, wo, g_post, gate, g_moe, sc, sh, wr_t, br, ts):
    bsz, s, d = x.shape
    half = o_mla.shape[2]
    tok = lambda width: pl.BlockSpec((1, ts, width), lambda b, i: (b, i, 0))
    per_b = pl.BlockSpec((1, 1, d), lambda b, i: (b, 0, 0))
    full = lambda a: pl.BlockSpec(a.shape, lambda b, i: (0,) * a.ndim)
    k_spec = pl.BlockSpec((1, TOP_K, ts), lambda b, i: (b, 0, i))
    return pl.pallas_call(
        _post_attn_kernel,
        grid=(bsz, s // ts),
        in_specs=[tok(half), tok(half), tok(d), full(wo), full(g_post), per_b, full(g_moe), per_b,
                  per_b, full(wr_t), full(br)],
        out_specs=[tok(d), tok(d // 2), k_spec, k_spec],
        out_shape=[jax.ShapeDtypeStruct((bsz, s, d), F32),
                   jax.ShapeDtypeStruct((bsz, s, d // 2), jnp.uint32),
                   jax.ShapeDtypeStruct((bsz, TOP_K, s), jnp.int32),
                   jax.ShapeDtypeStruct((bsz, TOP_K, s), F32)],
        compiler_params=_cparams(("arbitrary", "arbitrary")),
        name="post_attn",
    )(o_mla, o_fox, x, wo, g_post, gate, g_moe, sc, sh, wr_t, br)


def _route_kernel(idx_ref, tri_ref, rank_ref, cnt_ref, run_ref):
    first = jnp.logical_and(pl.program_id(0) == 0, pl.program_id(1) == 0)

    @pl.when(first)
    def _():
        run_ref[...] = jnp.zeros_like(run_ref)

    tr = idx_ref.shape[2]
    eid = lax.broadcasted_iota(jnp.int32, (N_EXPERTS, tr), 0)
    hot = [eid == idx_ref[0, kk:kk + 1, :] for kk in range(TOP_K)]
    multi = hot[0] | hot[1] | hot[2] | hot[3]
    mf = jnp.where(multi, 1.0, 0.0)
    incl = _dot(mf.astype(BF16), tri_ref[...]) + run_ref[:, 0:1]
    excl = incl - mf
    for kk in range(TOP_K):
        rank_ref[0, kk:kk + 1, :] = jnp.sum(jnp.where(hot[kk], excl, 0.0), axis=0,
                                            keepdims=True).astype(jnp.int32)
    run_ref[...] = jnp.broadcast_to(incl[:, tr - 1:tr], run_ref.shape)
    cnt_ref[...] = run_ref[...].astype(jnp.int32)


def _route(idx_kt, tr):
    bsz, _, s = idx_kt.shape
    tri = (jnp.arange(tr)[:, None] <= jnp.arange(tr)[None, :]).astype(BF16)
    k_spec = pl.BlockSpec((1, TOP_K, tr), lambda b, i: (b, 0, i))
    return pl.pallas_call(
        _route_kernel,
        grid=(bsz, s // tr),
        in_specs=[k_spec, pl.BlockSpec((tr, tr), lambda b, i: (0, 0))],
        out_specs=[k_spec, pl.BlockSpec((N_EXPERTS, LANES), lambda b, i: (0, 0))],
        out_shape=[jax.ShapeDtypeStruct((bsz, TOP_K, s), jnp.int32),
                   jax.ShapeDtypeStruct((N_EXPERTS, LANES), jnp.int32)],
        scratch_shapes=[pltpu.VMEM((N_EXPERTS, LANES), F32)],
        compiler_params=_cparams(("arbitrary", "arbitrary")),
        name="route",
    )(idx_kt, tri)


SC_CORES = 2
SC_SUBCORES = 16
SC_CHUNK = 64
MOE_GROUP_FRACTIONS = (1 / 18, 1 / 6, 1 / 3)


def _sc_gather(table, idx):
    m = idx.shape[0]
    width = table.shape[1]
    n_workers = SC_CORES * SC_SUBCORES
    per_worker = m // n_workers
    n_chunks = per_worker // SC_CHUNK
    assert n_chunks * SC_CHUNK * n_workers == m
    n_buf = next(nb for nb in (3, 2, 1) if n_chunks % nb == 0)
    mesh = plsc.VectorSubcoreMesh(core_axis_name="c", subcore_axis_name="s")

    @functools.partial(
        pl.kernel, mesh=mesh,
        out_type=jax.ShapeDtypeStruct((m, width), table.dtype),
        scratch_types=[pltpu.VMEM((n_chunks, SC_CHUNK), jnp.int32),
                       pltpu.VMEM((n_buf, SC_CHUNK, width), table.dtype),
                       pltpu.SemaphoreType.DMA((n_buf,)),
                       pltpu.SemaphoreType.DMA((n_buf,))],
        name="sc_gather",
    )
    def gather(table_hbm, idx_hbm, out_hbm, idx_v, rows_v, sem_g, sem_w):
        wid = lax.axis_index("s") * SC_CORES + lax.axis_index("c")
        base = wid * per_worker
        pltpu.sync_copy(idx_hbm.at[wid], idx_v)

        def gather_copy(j, b):
            return pltpu.make_async_copy(table_hbm.at[idx_v.at[j]], rows_v.at[b], sem_g.at[b])

        for b in range(n_buf):
            gather_copy(b, b).start()

        @pl.loop(0, n_chunks, step=n_buf)
        def _(g):
            for b in range(n_buf):
                j = g + b
                gather_copy(j, b).wait()
                off = pl.multiple_of(base + j * SC_CHUNK, SC_CHUNK)
                write = pltpu.make_async_copy(rows_v.at[b], out_hbm.at[pl.ds(off, SC_CHUNK)],
                                              sem_w.at[b])
                write.start()
                write.wait()

                @pl.when(j + n_buf < n_chunks)
                def _():
                    gather_copy(j + n_buf, b).start()

    return gather(table, idx.reshape(n_workers, n_chunks, SC_CHUNK))


def _expert_mlp(xb, wgu_bf, bgu_ref, wd_bf, bd_ref):
    ff = wd_bf.shape[0]
    gu = _dot(xb, wgu_bf[...]) + bgu_ref[0]
    g = jnp.minimum(gu[:, :ff], SWIGLU_LIMIT)
    u = jnp.clip(gu[:, ff:], -SWIGLU_LIMIT, SWIGLU_LIMIT)
    glu = g * jax.nn.sigmoid(SWIGLU_ALPHA * g)
    act = ((u + 1.0) * glu).astype(BF16)
    return _dot(act, wd_bf[...]) + bd_ref[0]


def _expert_pregathered_kernel(be_ref, first_ref, n_ref, sidx_ref, x_ref, wgu_ref, bgu_ref, wd_ref,
                               bd_ref, *rest):
    y_hbm, ybuf0, ybuf1, wgu_bf, wd_bf, sem_s = rest[-6:]
    i = pl.program_id(0)
    n = n_ref[0]
    bm = ybuf0.shape[0]
    ybufs = (ybuf0, ybuf1)

    def scatter_start(slot, r, priority=0):
        pltpu.make_async_copy(ybufs[slot].at[pl.ds(r, 1), :],
                              y_hbm.at[pl.ds(sidx_ref[0, 0, r], 1), :],
                              sem_s.at[slot]).start(priority)

    def scatter_wait(slot):
        pltpu.make_async_copy(ybufs[slot], y_hbm.at[pl.ds(0, bm), :], sem_s.at[slot]).wait()

    @pl.when(jnp.logical_and(first_ref[i] == 1, i < n))
    def _():
        wgu_bf[...] = wgu_ref[0].astype(BF16)
        wd_bf[...] = wd_ref[0].astype(BF16)

    def step(par):
        if par == 0:
            @pl.when(i == 0)
            def _():
                ybuf1[...] = jnp.zeros_like(ybuf1)

        @pl.when(jnp.logical_and(i >= 1, i <= n))
        def _():
            scatter_wait(par)

        @pl.when(i < n)
        def _():
            xb = _unpack_bf16_pairs(x_ref[...])
            for r in range(bm):
                scatter_start(1 - par, r, r % 2)
            ybufs[par][...] = _expert_mlp(xb, wgu_bf, bgu_ref, wd_bf, bd_ref)

        @pl.when(i == n)
        def _():
            def body(r, _):
                scatter_start(1 - par, r)
                return 0
            lax.fori_loop(0, bm, body, 0, unroll=8)
            scatter_wait(1 - par)

    @pl.when(i % 2 == 0)
    def _():
        step(0)

    @pl.when(i % 2 == 1)
    def _():
        step(1)


def _experts_pregathered(be_step, first_step, n_real, sidx, xs, w_gu, b_gu, w_d, b_d, y4, n_out_rows,
                         bm):
    e, d, f2 = w_gu.shape
    ff = w_d.shape[1]
    n_steps = sidx.shape[0]
    n_xblocks = xs.shape[0] // bm
    n_prefetch = 3
    prev = [] if y4 is None else [y4]
    return pl.pallas_call(
        _expert_pregathered_kernel,
        grid_spec=pltpu.PrefetchScalarGridSpec(
            num_scalar_prefetch=n_prefetch,
            grid=(n_steps,),
            in_specs=[
                pl.BlockSpec((1, 1, bm), lambda i, be, fi, nu: (i, 0, 0), memory_space=pltpu.SMEM),
                pl.BlockSpec((bm, d // 2), lambda i, be, fi, nu: (jnp.minimum(i, n_xblocks - 1), 0)),
                pl.BlockSpec((1, d, f2), lambda i, be, fi, nu: (be[i], 0, 0)),
                pl.BlockSpec((1, 1, f2), lambda i, be, fi, nu: (be[i], 0, 0)),
                pl.BlockSpec((1, ff, d), lambda i, be, fi, nu: (be[i], 0, 0)),
                pl.BlockSpec((1, 1, d), lambda i, be, fi, nu: (be[i], 0, 0)),
            ] + [pl.BlockSpec(memory_space=pl.ANY) for _ in prev],
            out_specs=pl.BlockSpec(memory_space=pl.ANY),
            scratch_shapes=[pltpu.VMEM((bm, d), F32), pltpu.VMEM((bm, d), F32),
                            pltpu.VMEM((d, f2), BF16), pltpu.VMEM((ff, d), BF16),
                            pltpu.SemaphoreType.DMA((2,))],
        ),
        out_shape=jax.ShapeDtypeStruct((n_out_rows, d), F32),
        input_output_aliases={n_prefetch + 6: 0} if prev else {},
        compiler_params=_cparams(("arbitrary",)),
        name="experts",
    )(be_step, first_step, n_real, sidx, xs, w_gu, b_gu.reshape(e, 1, f2), w_d, b_d.reshape(e, 1, d),
      *prev)


def _combine_kernel(y0_ref, y1_ref, y2_ref, y3_ref, gts_ref, x1_ref, gpost_ref, gate_ref, o_ref):
    gts = gts_ref[...]
    acc = gts[:, 0:1] * y0_ref[...]
    for kk, y_ref in enumerate((y1_ref, y2_ref, y3_ref), start=1):
        acc = acc + gts[:, kk:kk + 1] * y_ref[...]
    o_ref[0] = x1_ref[0] + gate_ref[0] * _rms(acc, gpost_ref[...])


def _combine(y4, gts, x1, g_post, gate, tt):
    bsz, s, d = x1.shape
    n_t = s // tt
    n_tiles = bsz * n_t
    y_spec = lambda kk: pl.BlockSpec((tt, d), lambda b, i: (kk * n_tiles + b * n_t + i, 0))
    return pl.pallas_call(
        _combine_kernel,
        grid=(bsz, n_t),
        in_specs=[
            y_spec(0), y_spec(1), y_spec(2), y_spec(3),
            pl.BlockSpec((tt, TOP_K), lambda b, i: (b * n_t + i, 0)),
            pl.BlockSpec((1, tt, d), lambda b, i: (b, i, 0)),
            pl.BlockSpec(g_post.shape, lambda b, i: (0, 0)),
            pl.BlockSpec((1, 1, d), lambda b, i: (b, 0, 0)),
        ],
        out_specs=pl.BlockSpec((1, tt, d), lambda b, i: (b, i, 0)),
        out_shape=jax.ShapeDtypeStruct((bsz, s, d), F32),
        compiler_params=_cparams(("arbitrary", "arbitrary")),
        name="combine",
    )(y4, y4, y4, y4, gts, x1, g_post, gate)


def _rope_patterns(positions):
    inv_freq = ROPE_THETA ** (-jnp.arange(0, ROPE_DIM, 2, dtype=F32) / ROPE_DIM)
    ang = positions.astype(F32)[..., None] * inv_freq
    cos, sin = jnp.cos(ang), jnp.sin(ang)
    ones = jnp.ones(positions.shape + (NOPE_DIM,), F32)
    zeros = jnp.zeros(positions.shape + (LANES - NOPE_DIM - ROPE_DIM,), F32)
    cp = jnp.concatenate([ones, cos, cos, zeros], axis=-1)
    sp = jnp.concatenate([0.0 * ones, -sin, sin, zeros], axis=-1)
    return cp, sp


def _prep_mixer_weights(w_in, w_uq, w_ukv, b_forget):
    d = w_in.shape[0]
    o_kr = Q_RANK + KV_RANK
    o_f = o_kr + ROPE_DIM
    hr = ROPE_DIM // 2
    z = lambda n: jnp.zeros((d, n), w_in.dtype)
    kr = w_in[:, o_kr:o_f]
    kr_sw = jnp.concatenate([kr[:, hr:], kr[:, :hr]], axis=1)
    pad = LANES - NOPE_DIM - ROPE_DIM
    w1 = jnp.concatenate([
        w_in[:, :o_kr],
        z(NOPE_DIM), kr, z(pad),
        z(NOPE_DIM), kr_sw, z(pad),
        w_in[:, o_f:o_f + 3 * FOX_WIDTH],
    ], axis=1).astype(BF16)
    wfl = jnp.zeros((FL_ROWS, d), w_in.dtype).at[:FOX_HEADS].set(
        w_in[:, o_f + 3 * FOX_WIDTH:].T).astype(BF16)
    bfg = jnp.zeros((FL_ROWS, 1), F32).at[:FOX_HEADS, 0].set(b_forget)

    wq = w_uq.reshape(Q_RANK, MLA_HEADS, NOPE_DIM + ROPE_DIM)
    nope, rope = wq[..., :NOPE_DIM], wq[..., NOPE_DIM:]
    zq = lambda n: jnp.zeros((Q_RANK, MLA_HEADS, n), w_uq.dtype)
    wqa = jnp.concatenate([nope, rope, zq(pad)], axis=-1).reshape(Q_RANK, -1).astype(BF16)
    wqb = jnp.concatenate([zq(NOPE_DIM), rope[..., hr:], rope[..., :hr], zq(pad)],
                          axis=-1).reshape(Q_RANK, -1).astype(BF16)
    wkv = w_ukv.reshape(KV_RANK, MLA_HEADS, NOPE_DIM + V_DIM)
    wuk = jnp.concatenate([wkv[..., :NOPE_DIM],
                           jnp.zeros((KV_RANK, MLA_HEADS, LANES - NOPE_DIM), w_ukv.dtype)],
                          axis=-1).reshape(KV_RANK, -1).astype(BF16)
    wuv = wkv[..., NOPE_DIM:].reshape(KV_RANK, -1).astype(BF16)
    return w1, wfl, bfg, wqa, wqb, wuk, wuv


def _slot_tables(idx_kt, rank_kt, counts, bm):
    bsz, _, s = idx_kt.shape
    t = bsz * s
    n_blocks = (t * TOP_K) // bm + N_EXPERTS
    n_real = t * TOP_K
    i32 = jnp.int32
    eids = jnp.arange(N_EXPERTS, dtype=i32)

    pcounts = ((counts + bm - 1) // bm) * bm
    incl_mat = (eids[:, None] <= eids[None, :]).astype(i32)
    pends = pcounts @ incl_mat
    pstarts = pends - pcounts
    cstarts = counts @ incl_mat - counts
    total = pends[-1]
    n_used = (total // bm).astype(i32).reshape(1)

    onehot = idx_kt[..., None] == eids
    dest = jnp.sum(jnp.where(onehot, pstarts, 0), axis=-1) + rank_kt
    tok = jnp.arange(t, dtype=i32).reshape(bsz, 1, s)
    kk = jnp.arange(TOP_K, dtype=i32).reshape(1, TOP_K, 1)
    yrow_real = jnp.broadcast_to(kk * t + tok, dest.shape)
    _, rows_sorted = lax.sort((dest.reshape(-1), yrow_real.reshape(-1)), num_keys=1)

    blk0 = jnp.arange(n_blocks, dtype=i32)
    be_blk = jnp.minimum(jnp.sum((pends[None, :] <= (blk0 * bm)[:, None]).astype(i32), axis=1),
                         N_EXPERTS - 1)
    hot_b = be_blk[:, None] == eids[None, :]
    pick = lambda v: jnp.sum(jnp.where(hot_b, v[None, :], 0), axis=1)[:, None]
    local = (blk0 * bm)[:, None] + jnp.arange(bm, dtype=i32)[None, :] - pick(pstarts)
    valid = jnp.logical_and(local < pick(counts), (blk0 < n_used[0])[:, None])
    j = jnp.clip(pick(cstarts) + local, 0, n_real - 1)
    rows = jnp.take(rows_sorted, j.reshape(-1), axis=0).reshape(n_blocks, bm)
    spare = TOP_K * t + jnp.arange(bm, dtype=i32)[None, :]
    yrow = jnp.where(valid, rows, spare)
    slot = (blk0 * bm)[:, None] + jnp.arange(bm, dtype=i32)[None, :]
    src_tok = jnp.where(valid, rows % t, slot % t)

    return src_tok, yrow, be_blk, n_used[0]


def _first_flags(be_step):
    return jnp.concatenate([jnp.ones((1,), jnp.int32),
                            (be_step[1:] != be_step[:-1]).astype(jnp.int32)])


def _group_sizes(n_blocks, bm):
    unit = 3 * (SC_CORES * SC_SUBCORES * SC_CHUNK) // bm
    sizes = [max(unit, (int(n_blocks * f) // unit) * unit) for f in MOE_GROUP_FRACTIONS]
    sizes.append(n_blocks - sum(sizes))
    assert sizes[-1] > 0 and sizes[-1] % unit == 0
    return sizes


def _moe_blocks(src_tok, yrow, be_blk, n_used, h2p, w_gu, b_gu, w_d, b_d, n_out_rows, bm):
    i32 = jnp.int32
    n_blocks = src_tok.shape[0]
    spare = (n_out_rows - bm) + jnp.arange(bm, dtype=i32)[None, :]
    y4 = None
    lo = 0
    for nb in _group_sizes(n_blocks, bm):
        xs = _sc_gather(h2p, src_tok[lo:lo + nb].reshape(-1))
        sidx = jnp.concatenate([spare, yrow[lo:lo + nb]]).reshape(nb + 1, 1, bm)
        be = be_blk[lo + jnp.minimum(jnp.arange(nb + 1, dtype=i32), nb - 1)]
        n_g = jnp.clip(n_used - lo, 0, nb).astype(i32).reshape(1)
        y4 = _experts_pregathered(be, _first_flags(be), n_g, sidx, xs, w_gu, b_gu, w_d, b_d, y4,
                                  n_out_rows, bm)
        lo += nb
    return y4


def _layer(x, ada, cp, sp, g_attn_pre, g_attn_post, w_in, g_q_norm, w_uq, g_kv_norm, w_ukv,
           b_forget, w_out, g_moe_pre, g_moe_post, w_router, b_router, w_gate_up, b_gate_up,
           w_down, b_down):
    bsz, s, d = x.shape
    t = bsz * s
    ts = min(512, s)
    tq = min(512, s)
    tt = min(512, s)
    tr = min(1024, s)
    bm = 512
    row = lambda v: v.reshape(1, -1)
    sh_a, sc_a, gt_a, sh_m, sc_m, gt_m = [v.reshape(bsz, 1, d) for v in jnp.split(ada, 6, axis=-1)]

    w1, wfl, bfg, wqa, wqb, wuk, wuv = _prep_mixer_weights(w_in, w_uq, w_ukv, b_forget)
    q, k, v, fq, fk, fv, fcum = _pre_attn(x, sc_a, sh_a, row(g_attn_pre), w1, wfl, bfg,
                                          row(g_q_norm), wqa, wqb, row(g_kv_norm), wuk, wuv,
                                          cp, sp, ts)
    o_mla = _attention(q, k, v, None, tq, LANES)
    fpairs = fcum[:, :FOX_HEADS].reshape(bsz, FOX_HEADS // 2, 2, s)
    o_fox = _attention(fq, fk, fv, fpairs, tq, FOX_DIM)

    x1, h2, idx_kt, gts_kt = _post_attn(o_mla, o_fox, x, w_out.astype(BF16), row(g_attn_post), gt_a,
                                        row(g_moe_pre), sc_m, sh_m, w_router.T.astype(BF16),
                                        b_router.reshape(-1, 1), ts)

    rank_kt, cnt = _route(idx_kt, tr)
    src_tok, yrow, be_blk, n_used = _slot_tables(idx_kt, rank_kt, cnt[:, 0], bm)
    y4 = _moe_blocks(src_tok, yrow, be_blk, n_used, h2.reshape(t, d // 2), w_gate_up, b_gate_up,
                     w_down, b_down, TOP_K * t + bm, bm)
    gts = gts_kt.transpose(0, 2, 1).reshape(t, TOP_K)
    return _combine(y4, gts, x1, row(g_moe_post), gt_m, tt)


def kernel(x, c, positions, w_ada, b_ada, g_attn_pre, g_attn_post, w_in, g_q_norm, w_uq, g_kv_norm,
           w_ukv, b_forget, w_out, g_moe_pre, g_moe_post, w_router, b_router, w_gate_up, b_gate_up,
           w_down, b_down):
    cp, sp = _rope_patterns(positions)
    for layer in range(w_ada.shape[0]):
        ada = _ada(c, w_ada[layer], b_ada[layer])
        x = _layer(x, ada, cp, sp, g_attn_pre[layer], g_attn_post[layer], w_in[layer],
                   g_q_norm[layer], w_uq[layer], g_kv_norm[layer], w_ukv[layer], b_forget[layer],
                   w_out[layer], g_moe_pre[layer], g_moe_post[layer], w_router[layer],
                   b_router[layer], w_gate_up[layer], b_gate_up[layer], w_down[layer],
                   b_down[layer])
    return x
```

```python
import functools
import math

import jax
import jax.numpy as jnp
from jax import lax
from jax.experimental import pallas as pl
from jax.experimental.pallas import tpu as pltpu
from jax.experimental.pallas import tpu_sc as plsc

F32 = jnp.float32
BF16 = jnp.bfloat16

MLA_HEADS = 8
NOPE_DIM = 64
ROPE_DIM = 32
V_DIM = 64
Q_RANK = 256
KV_RANK = 128
FOX_HEADS = 8
FOX_DIM = 64
FOX_WIDTH = FOX_HEADS * FOX_DIM
ROPE_THETA = 10000.0
N_EXPERTS = 32
TOP_K = 4
SWIGLU_ALPHA = 1.702
SWIGLU_LIMIT = 7.0
RMS_EPS = 1e-6
NEG_INF = -1e30
LOG2E = math.log2(math.e)
MLA_SCALE = LOG2E / math.sqrt(NOPE_DIM + ROPE_DIM)
FOX_SCALE = LOG2E / math.sqrt(FOX_DIM)

LANES = 128
C_CQ = (0, 256)
C_CKV = (256, 384)
C_KR = (384, 640)
C_FQ = (640, 1152)
C_FK = (1152, 1664)
C_FV = (1664, 2176)
W1_COLS = 2176
FL_ROWS = 16

VMEM_LIMIT = 56 * 1024 * 1024


def _cparams(sem):
    return pltpu.CompilerParams(dimension_semantics=sem, vmem_limit_bytes=VMEM_LIMIT)


def _rms(x, g):
    return x * lax.rsqrt(jnp.mean(x * x, axis=-1, keepdims=True) + RMS_EPS) * g


def _dot(a, b):
    return jnp.dot(a, b, preferred_element_type=F32)


def _dot_nt(a, b):
    return lax.dot_general(a, b, (((1,), (1,)), ((), ())), preferred_element_type=F32)


def _ada_kernel(c_ref, w_ref, b_ref, o_ref):
    c = c_ref[...]
    ca = (c * jax.nn.sigmoid(c)).astype(BF16)
    o_ref[...] = _dot(ca, w_ref[...].astype(BF16)) + b_ref[...]


def _ada(c, w, b):
    bsz, d = c.shape
    n = w.shape[1]
    bn = 1024
    return pl.pallas_call(
        _ada_kernel,
        grid=(n // bn,),
        in_specs=[
            pl.BlockSpec((bsz, d), lambda j: (0, 0)),
            pl.BlockSpec((d, bn), lambda j: (0, j)),
            pl.BlockSpec((1, bn), lambda j: (0, j)),
        ],
        out_specs=pl.BlockSpec((bsz, bn), lambda j: (0, j)),
        out_shape=jax.ShapeDtypeStruct((bsz, n), F32),
        compiler_params=_cparams(("arbitrary",)),
        name="ada",
    )(c, w, b.reshape(1, n))


def _pre_attn_kernel(x_ref, sc_ref, sh_ref, gpre_ref, w1_ref, wfl_ref, bf_ref, gq_ref, wqa_ref,
                     wqb_ref, gkv_ref, wuk_ref, wuv_ref, cp_ref, sp_ref,
                     q_ref, k_ref, v_ref, fq_ref, fk_ref, fv_ref, fc_ref, carry_ref):
    si = pl.program_id(1)
    ts = x_ref.shape[1]
    x = x_ref[0]
    h = _rms(x, gpre_ref[...]) * (1.0 + sc_ref[0]) + sh_ref[0]
    hb = h.astype(BF16)

    def proj(c):
        return _dot(hb, w1_ref[:, c[0]:c[1]])

    cp = cp_ref[0]
    sp = sp_ref[0]

    cqn = _rms(proj(C_CQ), gq_ref[...]).astype(BF16)
    qa = _dot(cqn, wqa_ref[...])
    qb = _dot(cqn, wqb_ref[...])
    for hd in range(MLA_HEADS):
        sl = slice(hd * LANES, (hd + 1) * LANES)
        q_ref[0, :, sl] = ((qa[:, sl] * cp + qb[:, sl] * sp) * MLA_SCALE).astype(BF16)

    kr = proj(C_KR)
    k_rope = kr[:, :LANES] * cp + kr[:, LANES:] * sp
    ckvn = _rms(proj(C_CKV), gkv_ref[...]).astype(BF16)
    kn = _dot(ckvn, wuk_ref[...])
    for hd in range(MLA_HEADS):
        sl = slice(hd * LANES, (hd + 1) * LANES)
        k_ref[0, :, sl] = (kn[:, sl] + k_rope).astype(BF16)
    v_ref[0] = _dot(ckvn, wuv_ref[...]).astype(BF16)

    fq_ref[0] = (proj(C_FQ) * FOX_SCALE).astype(BF16)
    fk_ref[0] = proj(C_FK).astype(BF16)
    fv_ref[0] = proj(C_FV).astype(BF16)

    fl = _dot_nt(wfl_ref[...], hb) + bf_ref[...]
    lf = jnp.minimum(fl, 0.0) - jnp.log1p(jnp.exp(-jnp.abs(fl)))
    r = lax.broadcasted_iota(jnp.int32, (ts, ts), 0)
    c = lax.broadcasted_iota(jnp.int32, (ts, ts), 1)
    tri = (r <= c).astype(BF16)
    p0 = lf.astype(BF16)
    r1 = lf - p0.astype(F32)
    p1 = r1.astype(BF16)
    p2 = (r1 - p1.astype(F32)).astype(BF16)
    cs = _dot(p0, tri) + _dot(p1, tri) + _dot(p2, tri)

    @pl.when(si == 0)
    def _():
        carry_ref[...] = jnp.zeros_like(carry_ref)

    cs = cs + carry_ref[:, 0:1]
    fc_ref[0] = cs * LOG2E
    carry_ref[...] = jnp.broadcast_to(cs[:, ts - 1:ts], carry_ref.shape)


def _pre_attn(x, sc, sh, g_pre, w1, wfl, bfg, g_q, wqa, wqb, g_kv, wuk, wuv, cp, sp, ts):
    bsz, s, d = x.shape
    grid = (bsz, s // ts)
    tok = lambda width: pl.BlockSpec((1, ts, width), lambda b, i: (b, i, 0))
    per_b = pl.BlockSpec((1, 1, d), lambda b, i: (b, 0, 0))
    full = lambda a: pl.BlockSpec(a.shape, lambda b, i: (0,) * a.ndim)
    outs = [
        jax.ShapeDtypeStruct((bsz, s, MLA_HEADS * LANES), BF16),
        jax.ShapeDtypeStruct((bsz, s, MLA_HEADS * LANES), BF16),
        jax.ShapeDtypeStruct((bsz, s, MLA_HEADS * V_DIM), BF16),
        jax.ShapeDtypeStruct((bsz, s, FOX_WIDTH), BF16),
        jax.ShapeDtypeStruct((bsz, s, FOX_WIDTH), BF16),
        jax.ShapeDtypeStruct((bsz, s, FOX_WIDTH), BF16),
        jax.ShapeDtypeStruct((bsz, FL_ROWS, s), F32),
    ]
    out_specs = [tok(MLA_HEADS * LANES), tok(MLA_HEADS * LANES), tok(MLA_HEADS * V_DIM),
                 tok(FOX_WIDTH), tok(FOX_WIDTH), tok(FOX_WIDTH),
                 pl.BlockSpec((1, FL_ROWS, ts), lambda b, i: (b, 0, i))]
    return pl.pallas_call(
        _pre_attn_kernel,
        grid=grid,
        in_specs=[tok(d), per_b, per_b, full(g_pre), full(w1), full(wfl), full(bfg), full(g_q),
                  full(wqa), full(wqb), full(g_kv), full(wuk), full(wuv), tok(LANES), tok(LANES)],
        out_specs=out_specs,
        out_shape=outs,
        scratch_shapes=[pltpu.VMEM((FL_ROWS, LANES), F32)],
        compiler_params=_cparams(("arbitrary", "arbitrary")),
        name="pre_attn",
    )(x, sc, sh, g_pre, w1, wfl, bfg, g_q, wqa, wqb, g_kv, wuk, wuv, cp, sp)


def _attn_kernel(*refs, tq, head_lanes, has_bias):
    if has_bias:
        q_ref, k_ref, v_ref, f_ref, o_ref, acc_ref, m_ref, l_ref = refs
    else:
        q_ref, k_ref, v_ref, o_ref, acc_ref, m_ref, l_ref = refs
        f_ref = None
    qi = pl.program_id(2)
    n_heads = acc_ref.shape[0]
    lane = lax.broadcasted_iota(jnp.int32, (tq, LANES), 1)
    qs = []
    for hd in range(n_heads):
        if head_lanes == LANES:
            qs.append(q_ref[0, :, hd * LANES:(hd + 1) * LANES])
        else:
            q2 = q_ref[0, :, (hd // 2) * LANES:(hd // 2 + 1) * LANES]
            keep = (lane < head_lanes) if hd % 2 == 0 else (lane >= head_lanes)
            qs.append(jnp.where(keep, q2, jnp.zeros_like(q2)))

    acc_ref[...] = jnp.zeros_like(acc_ref)
    m_ref[...] = jnp.full(m_ref.shape, NEG_INF, F32)
    l_ref[...] = jnp.zeros_like(l_ref)

    def step(off, width, masked):
        scores = []
        for hd in range(n_heads):
            pr = hd // 2
            if head_lanes == LANES:
                kk = k_ref[0, pl.ds(off, width), hd * LANES:(hd + 1) * LANES]
            else:
                kk = k_ref[0, pl.ds(off, width), pr * LANES:(pr + 1) * LANES]
            scores.append(_dot_nt(qs[hd], kk))
        for hd in range(n_heads):
            pr = hd // 2
            v2 = v_ref[0, pl.ds(off, width), pr * LANES:(pr + 1) * LANES]
            s = scores[hd]
            if has_bias:
                s = s - f_ref[0, pr, hd % 2:hd % 2 + 1, pl.ds(off, width)]
            if masked:
                rr = lax.broadcasted_iota(jnp.int32, (tq, width), 0)
                cc = lax.broadcasted_iota(jnp.int32, (tq, width), 1)
                s = jnp.where(cc <= rr, s, NEG_INF)
            m = m_ref[hd]
            m_new = jnp.maximum(m, jnp.max(s, axis=-1, keepdims=True))
            alpha = jnp.exp2(m - m_new)
            p = jnp.exp2(s - jnp.tile(m_new, (1, width // LANES)))
            l_ref[hd] = alpha * l_ref[hd] + jnp.sum(p, axis=-1, keepdims=True)
            m_ref[hd] = m_new
            acc_ref[hd] = alpha * acc_ref[hd] + _dot(p.astype(BF16), v2)

    def wide(j, _):
        step(pl.multiple_of(j * (2 * tq), 2 * tq), 2 * tq, False)
        return 0

    lax.fori_loop(0, qi // 2, wide, 0)

    @pl.when(qi % 2 == 1)
    def _():
        step(pl.multiple_of((qi - 1) * tq, tq), tq, False)

    step(pl.multiple_of(qi * tq, tq), tq, True)
    for pr in range(n_heads // 2):
        o0 = acc_ref[2 * pr] / l_ref[2 * pr]
        o1 = acc_ref[2 * pr + 1] / l_ref[2 * pr + 1]
        o_ref[0, :, pr * LANES:(pr + 1) * LANES] = jnp.where(lane < V_DIM, o0, o1).astype(o_ref.dtype)


ATTN_PAIRS = 2


def _attention(q, k, v, fcum, tq, head_lanes):
    bsz, s, _ = q.shape
    n_groups = v.shape[2] // (ATTN_PAIRS * LANES)
    qk_w = ATTN_PAIRS * 2 * head_lanes
    v_w = ATTN_PAIRS * LANES
    n_heads = 2 * ATTN_PAIRS
    has_bias = fcum is not None
    in_specs = [
        pl.BlockSpec((1, tq, qk_w), lambda b, p, i: (b, i, p)),
        pl.BlockSpec((1, s, qk_w), lambda b, p, i: (b, 0, p)),
        pl.BlockSpec((1, s, v_w), lambda b, p, i: (b, 0, p)),
    ]
    args = [q, k, v]
    if has_bias:
        in_specs.append(pl.BlockSpec((1, ATTN_PAIRS, 2, s), lambda b, p, i: (b, p, 0, 0)))
        args.append(fcum)
    return pl.pallas_call(
        functools.partial(_attn_kernel, tq=tq, head_lanes=head_lanes, has_bias=has_bias),
        grid=(bsz, n_groups, s // tq),
        in_specs=in_specs,
        out_specs=pl.BlockSpec((1, tq, v_w), lambda b, p, i: (b, i, p)),
        out_shape=jax.ShapeDtypeStruct((bsz, s, v.shape[2]), BF16),
        scratch_shapes=[pltpu.VMEM((n_heads, tq, LANES), F32), pltpu.VMEM((n_heads, tq, LANES), F32),
                        pltpu.VMEM((n_heads, tq, LANES), F32)],
        compiler_params=_cparams(("arbitrary", "arbitrary", "arbitrary")),
        name="attn_fox" if has_bias else "attn_mla",
    )(*args)


def _pack_bf16_pairs(xb):
    w = xb.shape[1] // 2
    lo = lax.bitcast_convert_type(xb[:, :w].astype(F32), jnp.uint32)
    hi = lax.bitcast_convert_type(xb[:, w:].astype(F32), jnp.uint32)
    return (hi & jnp.uint32(0xFFFF0000)) | lax.shift_right_logical(lo, jnp.uint32(16))


def _unpack_bf16_pairs(xw):
    lo = lax.bitcast_convert_type(lax.shift_left(xw, jnp.uint32(16)), F32)
    hi = lax.bitcast_convert_type(xw & jnp.uint32(0xFFFF0000), F32)
    return jnp.concatenate([lo, hi], axis=1).astype(BF16)


def _post_attn_kernel(om_ref, of_ref, x_ref, wo_ref, gpost_ref, gate_ref, gmoe_ref, sc_ref, sh_ref,
                      wr_ref, br_ref, x1_ref, h2_ref, idx_ref, gts_ref):
    half = om_ref.shape[2]
    o = _dot(om_ref[0], wo_ref[0:half, :]) + _dot(of_ref[0], wo_ref[half:, :])
    x1 = x_ref[0] + gate_ref[0] * _rms(o, gpost_ref[...])
    x1_ref[0] = x1
    h2 = _rms(x1, gmoe_ref[...]) * (1.0 + sc_ref[0]) + sh_ref[0]
    hb = h2.astype(BF16)
    h2_ref[0] = _pack_bf16_pairs(hb)
    logits = _dot_nt(wr_ref[...], hb) + br_ref[...]
    n_e = logits.shape[0]
    eid = lax.broadcasted_iota(jnp.int32, logits.shape, 0)
    vals, idxs = [], []
    for _ in range(TOP_K):
        m = jnp.max(logits, axis=0, keepdims=True)
        ix = jnp.min(jnp.where(logits == m, eid, n_e), axis=0, keepdims=True)
        vals.append(m)
        idxs.append(ix)
        logits = jnp.where(eid == ix, -jnp.inf, logits)
    es = [jnp.exp(vv - vals[0]) for vv in vals]
    den = es[0] + es[1] + es[2] + es[3]
    for kk in range(TOP_K):
        idx_ref[0, kk:kk + 1, :] = idxs[kk]
        gts_ref[0, kk:kk + 1, :] = es[kk] / den


def _post_attn(o_mla, o_fox, x, wo, g_post, gate, g_moe, sc, sh, wr_t, br, ts):
    bsz, s, d = x.shape
    half = o_mla.shape[2]
    tok = lambda width: pl.BlockSpec((1, ts, width), lambda b, i: (b, i, 0))
    per_b = pl.BlockSpec((1, 1, d), lambda b, i: (b, 0, 0))
    full = lambda a: pl.BlockSpec(a.shape, lambda b, i: (0,) * a.ndim)
    k_spec = pl.BlockSpec((1, TOP_K, ts), lambda b, i: (b, 0, i))
    return pl.pallas_call(
        _post_attn_kernel,
        grid=(bsz, s // ts),
        in_specs=[tok(half), tok(half), tok(d), full(wo), full(g_post), per_b, full(g_moe), per_b,
                  per_b, full(wr_t), full(br)],
        out_specs=[tok(d), tok(d // 2), k_spec, k_spec],
        out_shape=[jax.ShapeDtypeStruct((bsz, s, d), F32),
                   jax.ShapeDtypeStruct((bsz, s, d // 2), jnp.uint32),
                   jax.ShapeDtypeStruct((bsz, TOP_K, s), jnp.int32),
                   jax.ShapeDtypeStruct((bsz, TOP_K, s), F32)],
        compiler_params=_cparams(("arbitrary", "arbitrary")),
        name="post_attn",
    )(o_mla, o_fox, x, wo, g_post, gate, g_moe, sc, sh, wr_t, br)


def _route_kernel(idx_ref, tri_ref, rank_ref, cnt_ref, run_ref):
    first = jnp.logical_and(pl.program_id(0) == 0, pl.program_id(1) == 0)

    @pl.when(first)
    def _():
        run_ref[...] = jnp.zeros_like(run_ref)

    tr = idx_ref.shape[2]
    eid = lax.broadcasted_iota(jnp.int32, (N_EXPERTS, tr), 0)
    hot = [eid == idx_ref[0, kk:kk + 1, :] for kk in range(TOP_K)]
    multi = hot[0] | hot[1] | hot[2] | hot[3]
    mf = jnp.where(multi, 1.0, 0.0)
    incl = _dot(mf.astype(BF16), tri_ref[...]) + run_ref[:, 0:1]
    excl = incl - mf
    for kk in range(TOP_K):
        rank_ref[0, kk:kk + 1, :] = jnp.sum(jnp.where(hot[kk], excl, 0.0), axis=0,
                                            keepdims=True).astype(jnp.int32)
    run_ref[...] = jnp.broadcast_to(incl[:, tr - 1:tr], run_ref.shape)
    cnt_ref[...] = run_ref[...].astype(jnp.int32)


def _route(idx_kt, tr):
    bsz, _, s = idx_kt.shape
    tri = (jnp.arange(tr)[:, None] <= jnp.arange(tr)[None, :]).astype(BF16)
    k_spec = pl.BlockSpec((1, TOP_K, tr), lambda b, i: (b, 0, i))
    return pl.pallas_call(
        _route_kernel,
        grid=(bsz, s // tr),
        in_specs=[k_spec, pl.BlockSpec((tr, tr), lambda b, i: (0, 0))],
        out_specs=[k_spec, pl.BlockSpec((N_EXPERTS, LANES), lambda b, i: (0, 0))],
        out_shape=[jax.ShapeDtypeStruct((bsz, TOP_K, s), jnp.int32),
                   jax.ShapeDtypeStruct((N_EXPERTS, LANES), jnp.int32)],
        scratch_shapes=[pltpu.VMEM((N_EXPERTS, LANES), F32)],
        compiler_params=_cparams(("arbitrary", "arbitrary")),
        name="route",
    )(idx_kt, tri)


SC_CORES = 2
SC_SUBCORES = 16
SC_CHUNK = 64
MOE_GROUP_FRACTIONS = (1 / 18, 1 / 6, 1 / 3)


def _sc_gather(table, idx):
    m = idx.shape[0]
    width = table.shape[1]
    n_workers = SC_CORES * SC_SUBCORES
    per_worker = m // n_workers
    n_chunks = per_worker // SC_CHUNK
    assert n_chunks * SC_CHUNK * n_workers == m
    n_buf = next(nb for nb in (3, 2, 1) if n_chunks % nb == 0)
    mesh = plsc.VectorSubcoreMesh(core_axis_name="c", subcore_axis_name="s")

    @functools.partial(
        pl.kernel, mesh=mesh,
        out_type=jax.ShapeDtypeStruct((m, width), table.dtype),
        scratch_types=[pltpu.VMEM((n_chunks, SC_CHUNK), jnp.int32),
                       pltpu.VMEM((n_buf, SC_CHUNK, width), table.dtype),
                       pltpu.SemaphoreType.DMA((n_buf,)),
                       pltpu.SemaphoreType.DMA((n_buf,))],
        name="sc_gather",
    )
    def gather(table_hbm, idx_hbm, out_hbm, idx_v, rows_v, sem_g, sem_w):
        wid = lax.axis_index("s") * SC_CORES + lax.axis_index("c")
        base = wid * per_worker
        pltpu.sync_copy(idx_hbm.at[wid], idx_v)

        def gather_copy(j, b):
            return pltpu.make_async_copy(table_hbm.at[idx_v.at[j]], rows_v.at[b], sem_g.at[b])

        for b in range(n_buf):
            gather_copy(b, b).start()

        @pl.loop(0, n_chunks, step=n_buf)
        def _(g):
            for b in range(n_buf):
                j = g + b
                gather_copy(j, b).wait()
                off = pl.multiple_of(base + j * SC_CHUNK, SC_CHUNK)
                write = pltpu.make_async_copy(rows_v.at[b], out_hbm.at[pl.ds(off, SC_CHUNK)],
                                              sem_w.at[b])
                write.start()
                write.wait()

                @pl.when(j + n_buf < n_chunks)
                def _():
                    gather_copy(j + n_buf, b).start()

    return gather(table, idx.reshape(n_workers, n_chunks, SC_CHUNK))


def _expert_mlp(xb, wgu_bf, bgu_ref, wd_bf, bd_ref):
    ff = wd_bf.shape[0]
    gu = _dot(xb, wgu_bf[...]) + bgu_ref[0]
    g = jnp.minimum(gu[:, :ff], SWIGLU_LIMIT)
    u = jnp.clip(gu[:, ff:], -SWIGLU_LIMIT, SWIGLU_LIMIT)
    glu = g * jax.nn.sigmoid(SWIGLU_ALPHA * g)
    act = ((u + 1.0) * glu).astype(BF16)
    return _dot(act, wd_bf[...]) + bd_ref[0]


def _expert_pregathered_kernel(be_ref, first_ref, n_ref, sidx_ref, x_ref, wgu_ref, bgu_ref, wd_ref,
                               bd_ref, *rest):
    y_hbm, ybuf0, ybuf1, wgu_bf, wd_bf, sem_s = rest[-6:]
    i = pl.program_id(0)
    n = n_ref[0]
    bm = ybuf0.shape[0]
    ybufs = (ybuf0, ybuf1)

    def scatter_start(slot, r, priority=0):
        pltpu.make_async_copy(ybufs[slot].at[pl.ds(r, 1), :],
                              y_hbm.at[pl.ds(sidx_ref[0, 0, r], 1), :],
                              sem_s.at[slot]).start(priority)

    def scatter_wait(slot):
        pltpu.make_async_copy(ybufs[slot], y_hbm.at[pl.ds(0, bm), :], sem_s.at[slot]).wait()

    @pl.when(jnp.logical_and(first_ref[i] == 1, i < n))
    def _():
        wgu_bf[...] = wgu_ref[0].astype(BF16)
        wd_bf[...] = wd_ref[0].astype(BF16)

    def step(par):
        if par == 0:
            @pl.when(i == 0)
            def _():
                ybuf1[...] = jnp.zeros_like(ybuf1)

        @pl.when(jnp.logical_and(i >= 1, i <= n))
        def _():
            scatter_wait(par)

        @pl.when(i < n)
        def _():
            xb = _unpack_bf16_pairs(x_ref[...])
            for r in range(bm):
                scatter_start(1 - par, r, r % 2)
            ybufs[par][...] = _expert_mlp(xb, wgu_bf, bgu_ref, wd_bf, bd_ref)

        @pl.when(i == n)
        def _():
            def body(r, _):
                scatter_start(1 - par, r)
                return 0
            lax.fori_loop(0, bm, body, 0, unroll=8)
            scatter_wait(1 - par)

    @pl.when(i % 2 == 0)
    def _():
        step(0)

    @pl.when(i % 2 == 1)
    def _():
        step(1)


def _experts_pregathered(be_step, first_step, n_real, sidx, xs, w_gu, b_gu, w_d, b_d, y4, n_out_rows,
                         bm):
    e, d, f2 = w_gu.shape
    ff = w_d.shape[1]
    n_steps = sidx.shape[0]
    n_xblocks = xs.shape[0] // bm
    n_prefetch = 3
    prev = [] if y4 is None else [y4]
    return pl.pallas_call(
        _expert_pregathered_kernel,
        grid_spec=pltpu.PrefetchScalarGridSpec(
            num_scalar_prefetch=n_prefetch,
            grid=(n_steps,),
            in_specs=[
                pl.BlockSpec((1, 1, bm), lambda i, be, fi, nu: (i, 0, 0), memory_space=pltpu.SMEM),
                pl.BlockSpec((bm, d // 2), lambda i, be, fi, nu: (jnp.minimum(i, n_xblocks - 1), 0)),
                pl.BlockSpec((1, d, f2), lambda i, be, fi, nu: (be[i], 0, 0)),
                pl.BlockSpec((1, 1, f2), lambda i, be, fi, nu: (be[i], 0, 0)),
                pl.BlockSpec((1, ff, d), lambda i, be, fi, nu: (be[i], 0, 0)),
                pl.BlockSpec((1, 1, d), lambda i, be, fi, nu: (be[i], 0, 0)),
            ] + [pl.BlockSpec(memory_space=pl.ANY) for _ in prev],
            out_specs=pl.BlockSpec(memory_space=pl.ANY),
            scratch_shapes=[pltpu.VMEM((bm, d), F32), pltpu.VMEM((bm, d), F32),
                            pltpu.VMEM((d, f2), BF16), pltpu.VMEM((ff, d), BF16),
                            pltpu.SemaphoreType.DMA((2,))],
        ),
        out_shape=jax.ShapeDtypeStruct((n_out_rows, d), F32),
        input_output_aliases={n_prefetch + 6: 0} if prev else {},
        compiler_params=_cparams(("arbitrary",)),
        name="experts",
    )(be_step, first_step, n_real, sidx, xs, w_gu, b_gu.reshape(e, 1, f2), w_d, b_d.reshape(e, 1, d),
      *prev)


def _combine_kernel(y0_ref, y1_ref, y2_ref, y3_ref, gts_ref, x1_ref, gpost_ref, gate_ref, o_ref):
    gts = gts_ref[...]
    acc = gts[:, 0:1] * y0_ref[...]
    for kk, y_ref in enumerate((y1_ref, y2_ref, y3_ref), start=1):
        acc = acc + gts[:, kk:kk + 1] * y_ref[...]
    o_ref[0] = x1_ref[0] + gate_ref[0] * _rms(acc, gpost_ref[...])


def _combine(y4, gts, x1, g_post, gate, tt):
    bsz, s, d = x1.shape
    n_t = s // tt
    n_tiles = bsz * n_t
    y_spec = lambda kk: pl.BlockSpec((tt, d), lambda b, i: (kk * n_tiles + b * n_t + i, 0))
    return pl.pallas_call(
        _combine_kernel,
        grid=(bsz, n_t),
        in_specs=[
            y_spec(0), y_spec(1), y_spec(2), y_spec(3),
            pl.BlockSpec((tt, TOP_K), lambda b, i: (b * n_t + i, 0)),
            pl.BlockSpec((1, tt, d), lambda b, i: (b, i, 0)),
            pl.BlockSpec(g_post.shape, lambda b, i: (0, 0)),
            pl.BlockSpec((1, 1, d), lambda b, i: (b, 0, 0)),
        ],
        out_specs=pl.BlockSpec((1, tt, d), lambda b, i: (b, i, 0)),
        out_shape=jax.ShapeDtypeStruct((bsz, s, d), F32),
        compiler_params=_cparams(("arbitrary", "arbitrary")),
        name="combine",
    )(y4, y4, y4, y4, gts, x1, g_post, gate)


def _rope_patterns(positions):
    inv_freq = ROPE_THETA ** (-jnp.arange(0, ROPE_DIM, 2, dtype=F32) / ROPE_DIM)
    ang = positions.astype(F32)[..., None] * inv_freq
    cos, sin = jnp.cos(ang), jnp.sin(ang)
    ones = jnp.ones(positions.shape + (NOPE_DIM,), F32)
    zeros = jnp.zeros(positions.shape + (LANES - NOPE_DIM - ROPE_DIM,), F32)
    cp = jnp.concatenate([ones, cos, cos, zeros], axis=-1)
    sp = jnp.concatenate([0.0 * ones, -sin, sin, zeros], axis=-1)
    return cp, sp


def _prep_mixer_weights(w_in, w_uq, w_ukv, b_forget):
    d = w_in.shape[0]
    o_kr = Q_RANK + KV_RANK
    o_f = o_kr + ROPE_DIM
    hr = ROPE_DIM // 2
    z = lambda n: jnp.zeros((d, n), w_in.dtype)
    kr = w_in[:, o_kr:o_f]
    kr_sw = jnp.concatenate([kr[:, hr:], kr[:, :hr]], axis=1)
    pad = LANES - NOPE_DIM - ROPE_DIM
    w1 = jnp.concatenate([
        w_in[:, :o_kr],
        z(NOPE_DIM), kr, z(pad),
        z(NOPE_DIM), kr_sw, z(pad),
        w_in[:, o_f:o_f + 3 * FOX_WIDTH],
    ], axis=1).astype(BF16)
    wfl = jnp.zeros((FL_ROWS, d), w_in.dtype).at[:FOX_HEADS].set(
        w_in[:, o_f + 3 * FOX_WIDTH:].T).astype(BF16)
    bfg = jnp.zeros((FL_ROWS, 1), F32).at[:FOX_HEADS, 0].set(b_forget)

    wq = w_uq.reshape(Q_RANK, MLA_HEADS, NOPE_DIM + ROPE_DIM)
    nope, rope = wq[..., :NOPE_DIM], wq[..., NOPE_DIM:]
    zq = lambda n: jnp.zeros((Q_RANK, MLA_HEADS, n), w_uq.dtype)
    wqa = jnp.concatenate([nope, rope, zq(pad)], axis=-1).reshape(Q_RANK, -1).astype(BF16)
    wqb = jnp.concatenate([zq(NOPE_DIM), rope[..., hr:], rope[..., :hr], zq(pad)],
                          axis=-1).reshape(Q_RANK, -1).astype(BF16)
    wkv = w_ukv.reshape(KV_RANK, MLA_HEADS, NOPE_DIM + V_DIM)
    wuk = jnp.concatenate([wkv[..., :NOPE_DIM],
                           jnp.zeros((KV_RANK, MLA_HEADS, LANES - NOPE_DIM), w_ukv.dtype)],
                          axis=-1).reshape(KV_RANK, -1).astype(BF16)
    wuv = wkv[..., NOPE_DIM:].reshape(KV_RANK, -1).astype(BF16)
    return w1, wfl, bfg, wqa, wqb, wuk, wuv


def _slot_tables(idx_kt, rank_kt, counts, bm):
    bsz, _, s = idx_kt.shape
    t = bsz * s
    n_blocks = (t * TOP_K) // bm + N_EXPERTS
    n_real = t * TOP_K
    i32 = jnp.int32
    eids = jnp.arange(N_EXPERTS, dtype=i32)

    pcounts = ((counts + bm - 1) // bm) * bm
    incl_mat = (eids[:, None] <= eids[None, :]).astype(i32)
    pends = pcounts @ incl_mat
    pstarts = pends - pcounts
    cstarts = counts @ incl_mat - counts
    total = pends[-1]
    n_used = (total // bm).astype(i32).reshape(1)

    onehot = idx_kt[..., None] == eids
    dest = jnp.sum(jnp.where(onehot, pstarts, 0), axis=-1) + rank_kt
    tok = jnp.arange(t, dtype=i32).reshape(bsz, 1, s)
    kk = jnp.arange(TOP_K, dtype=i32).reshape(1, TOP_K, 1)
    yrow_real = jnp.broadcast_to(kk * t + tok, dest.shape)
    _, rows_sorted = lax.sort((dest.reshape(-1), yrow_real.reshape(-1)), num_keys=1)

    blk0 = jnp.arange(n_blocks, dtype=i32)
    be_blk = jnp.minimum(jnp.sum((pends[None, :] <= (blk0 * bm)[:, None]).astype(i32), axis=1),
                         N_EXPERTS - 1)
    hot_b = be_blk[:, None] == eids[None, :]
    pick = lambda v: jnp.sum(jnp.where(hot_b, v[None, :], 0), axis=1)[:, None]
    local = (blk0 * bm)[:, None] + jnp.arange(bm, dtype=i32)[None, :] - pick(pstarts)
    valid = jnp.logical_and(local < pick(counts), (blk0 < n_used[0])[:, None])
    j = jnp.clip(pick(cstarts) + local, 0, n_real - 1)
    rows = jnp.take(rows_sorted, j.reshape(-1), axis=0).reshape(n_blocks, bm)
    spare = TOP_K * t + jnp.arange(bm, dtype=i32)[None, :]
    yrow = jnp.where(valid, rows, spare)
    slot = (blk0 * bm)[:, None] + jnp.arange(bm, dtype=i32)[None, :]
    src_tok = jnp.where(valid, rows % t, slot % t)

    return src_tok, yrow, be_blk, n_used[0]


def _first_flags(be_step):
    return jnp.concatenate([jnp.ones((1,), jnp.int32),
                            (be_step[1:] != be_step[:-1]).astype(jnp.int32)])


def _group_sizes(n_blocks, bm):
    unit = 3 * (SC_CORES * SC_SUBCORES * SC_CHUNK) // bm
    sizes = [max(unit, (int(n_blocks * f) // unit) * unit) for f in MOE_GROUP_FRACTIONS]
    sizes.append(n_blocks - sum(sizes))
    assert sizes[-1] > 0 and sizes[-1] % unit == 0
    return sizes


def _moe_blocks(src_tok, yrow, be_blk, n_used, h2p, w_gu, b_gu, w_d, b_d, n_out_rows, bm):
    i32 = jnp.int32
    n_blocks = src_tok.shape[0]
    spare = (n_out_rows - bm) + jnp.arange(bm, dtype=i32)[None, :]
    y4 = None
    lo = 0
    for nb in _group_sizes(n_blocks, bm):
        xs = _sc_gather(h2p, src_tok[lo:lo + nb].reshape(-1))
        sidx = jnp.concatenate([spare, yrow[lo:lo + nb]]).reshape(nb + 1, 1, bm)
        be = be_blk[lo + jnp.minimum(jnp.arange(nb + 1, dtype=i32), nb - 1)]
        n_g = jnp.clip(n_used - lo, 0, nb).astype(i32).reshape(1)
        y4 = _experts_pregathered(be, _first_flags(be), n_g, sidx, xs, w_gu, b_gu, w_d, b_d, y4,
                                  n_out_rows, bm)
        lo += nb
    return y4


def _layer(x, ada, cp, sp, g_attn_pre, g_attn_post, w_in, g_q_norm, w_uq, g_kv_norm, w_ukv,
           b_forget, w_out, g_moe_pre, g_moe_post, w_router, b_router, w_gate_up, b_gate_up,
           w_down, b_down):
    bsz, s, d = x.shape
    t = bsz * s
    ts = min(512, s)
    tq = min(512, s)
    tt = min(512, s)
    tr = min(1024, s)
    bm = 512
    row = lambda v: v.reshape(1, -1)
    sh_a, sc_a, gt_a, sh_m, sc_m, gt_m = [v.reshape(bsz, 1, d) for v in jnp.split(ada, 6, axis=-1)]

    w1, wfl, bfg, wqa, wqb, wuk, wuv = _prep_mixer_weights(w_in, w_uq, w_ukv, b_forget)
    q, k, v, fq, fk, fv, fcum = _pre_attn(x, sc_a, sh_a, row(g_attn_pre), w1, wfl, bfg,
                                          row(g_q_norm), wqa, wqb, row(g_kv_norm), wuk, wuv,
                                          cp, sp, ts)
    o_mla = _attention(q, k, v, None, tq, LANES)
    fpairs = fcum[:, :FOX_HEADS].reshape(bsz, FOX_HEADS // 2, 2, s)
    o_fox = _attention(fq, fk, fv, fpairs, tq, FOX_DIM)

    x1, h2, idx_kt, gts_kt = _post_attn(o_mla, o_fox, x, w_out.astype(BF16), row(g_attn_post), gt_a,
                                        row(g_moe_pre), sc_m, sh_m, w_router.T.astype(BF16),
                                        b_router.reshape(-1, 1), ts)

    rank_kt, cnt = _route(idx_kt, tr)
    src_tok, yrow, be_blk, n_used = _slot_tables(idx_kt, rank_kt, cnt[:, 0], bm)
    y4 = _moe_blocks(src_tok, yrow, be_blk, n_used, h2.reshape(t, d // 2), w_gate_up, b_gate_up,
                     w_down, b_down, TOP_K * t + bm, bm)
    gts = gts_kt.transpose(0, 2, 1).reshape(t, TOP_K)
    return _combine(y4, gts, x1, row(g_moe_post), gt_m, tt)


def kernel(x, c, positions, w_ada, b_ada, g_attn_pre, g_attn_post, w_in, g_q_norm, w_uq, g_kv_norm,
           w_ukv, b_forget, w_out, g_moe_pre, g_moe_post, w_router, b_router, w_gate_up, b_gate_up,
           w_down, b_down):
    cp, sp = _rope_patterns(positions)
    for layer in range(w_ada.shape[0]):
        ada = _ada(c, w_ada[layer], b_ada[layer])
        x = _layer(x, ada, cp, sp, g_attn_pre[layer], g_attn_post[layer], w_in[layer],
                   g_q_norm[layer], w_uq[layer], g_kv_norm[layer], w_ukv[layer], b_forget[layer],
                   w_out[layer], g_moe_pre[layer], g_moe_post[layer], w_router[layer],
                   b_router[layer], w_gate_up[layer], b_gate_up[layer], w_down[layer],
                   b_down[layer])
    return x
```

```python
import functools
import math

import jax
import jax.numpy as jnp
from jax import lax
from jax.experimental import pallas as pl
from jax.experimental.pallas import tpu as pltpu
from jax.experimental.pallas import tpu_sc as plsc

F32 = jnp.float32
BF16 = jnp.bfloat16

MLA_HEADS = 8
NOPE_DIM = 64
ROPE_DIM = 32
V_DIM = 64
Q_RANK = 256
KV_RANK = 128
FOX_HEADS = 8
FOX_DIM = 64
FOX_WIDTH = FOX_HEADS * FOX_DIM
ROPE_THETA = 10000.0
N_EXPERTS = 32
TOP_K = 4
SWIGLU_ALPHA = 1.702
SWIGLU_LIMIT = 7.0
RMS_EPS = 1e-6
NEG_INF = -1e30
LOG2E = math.log2(math.e)
MLA_SCALE = LOG2E / math.sqrt(NOPE_DIM + ROPE_DIM)
FOX_SCALE = LOG2E / math.sqrt(FOX_DIM)

LANES = 128
C_CQ = (0, 256)
C_CKV = (256, 384)
C_KR = (384, 640)
C_FQ = (640, 1152)
C_FK = (1152, 1664)
C_FV = (1664, 2176)
W1_COLS = 2176
FL_ROWS = 16

VMEM_LIMIT = 56 * 1024 * 1024


def _cparams(sem):
    return pltpu.CompilerParams(dimension_semantics=sem, vmem_limit_bytes=VMEM_LIMIT)


def _rms(x, g):
    return x * lax.rsqrt(jnp.mean(x * x, axis=-1, keepdims=True) + RMS_EPS) * g


def _dot(a, b):
    return jnp.dot(a, b, preferred_element_type=F32)


def _dot_nt(a, b):
    return lax.dot_general(a, b, (((1,), (1,)), ((), ())), preferred_element_type=F32)


def _ada_kernel(c_ref, w_ref, b_ref, o_ref):
    c = c_ref[...]
    ca = (c * jax.nn.sigmoid(c)).astype(BF16)
    o_ref[...] = _dot(ca, w_ref[...].astype(BF16)) + b_ref[...]


def _ada(c, w, b):
    bsz, d = c.shape
    n = w.shape[1]
    bn = 1024
    return pl.pallas_call(
        _ada_kernel,
        grid=(n // bn,),
        in_specs=[
            pl.BlockSpec((bsz, d), lambda j: (0, 0)),
            pl.BlockSpec((d, bn), lambda j: (0, j)),
            pl.BlockSpec((1, bn), lambda j: (0, j)),
        ],
        out_specs=pl.BlockSpec((bsz, bn), lambda j: (0, j)),
        out_shape=jax.ShapeDtypeStruct((bsz, n), F32),
        compiler_params=_cparams(("arbitrary",)),
        name="ada",
    )(c, w, b.reshape(1, n))


def _pre_attn_kernel(x_ref, sc_ref, sh_ref, gpre_ref, w1_ref, wfl_ref, bf_ref, gq_ref, wqa_ref,
                     wqb_ref, gkv_ref, wuk_ref, wuv_ref, cp_ref, sp_ref,
                     q_ref, k_ref, v_ref, fq_ref, fk_ref, fv_ref, fc_ref, carry_ref):
    si = pl.program_id(1)
    ts = x_ref.shape[1]
    x = x_ref[0]
    h = _rms(x, gpre_ref[...]) * (1.0 + sc_ref[0]) + sh_ref[0]
    hb = h.astype(BF16)

    def proj(c):
        return _dot(hb, w1_ref[:, c[0]:c[1]])

    cp = cp_ref[0]
    sp = sp_ref[0]

    cqn = _rms(proj(C_CQ), gq_ref[...]).astype(BF16)
    qa = _dot(cqn, wqa_ref[...])
    qb = _dot(cqn, wqb_ref[...])
    for hd in range(MLA_HEADS):
        sl = slice(hd * LANES, (hd + 1) * LANES)
        q_ref[0, :, sl] = ((qa[:, sl] * cp + qb[:, sl] * sp) * MLA_SCALE).astype(BF16)

    kr = proj(C_KR)
    k_rope = kr[:, :LANES] * cp + kr[:, LANES:] * sp
    ckvn = _rms(proj(C_CKV), gkv_ref[...]).astype(BF16)
    kn = _dot(ckvn, wuk_ref[...])
    for hd in range(MLA_HEADS):
        sl = slice(hd * LANES, (hd + 1) * LANES)
        k_ref[0, :, sl] = (kn[:, sl] + k_rope).astype(BF16)
    v_ref[0] = _dot(ckvn, wuv_ref[...]).astype(BF16)

    fq_ref[0] = (proj(C_FQ) * FOX_SCALE).astype(BF16)
    fk_ref[0] = proj(C_FK).astype(BF16)
    fv_ref[0] = proj(C_FV).astype(BF16)

    fl = _dot_nt(wfl_ref[...], hb) + bf_ref[...]
    lf = jnp.minimum(fl, 0.0) - jnp.log1p(jnp.exp(-jnp.abs(fl)))
    r = lax.broadcasted_iota(jnp.int32, (ts, ts), 0)
    c = lax.broadcasted_iota(jnp.int32, (ts, ts), 1)
    tri = (r <= c).astype(BF16)
    p0 = lf.astype(BF16)
    r1 = lf - p0.astype(F32)
    p1 = r1.astype(BF16)
    p2 = (r1 - p1.astype(F32)).astype(BF16)
    cs = _dot(p0, tri) + _dot(p1, tri) + _dot(p2, tri)

    @pl.when(si == 0)
    def _():
        carry_ref[...] = jnp.zeros_like(carry_ref)

    cs = cs + carry_ref[:, 0:1]
    fc_ref[0] = cs * LOG2E
    carry_ref[...] = jnp.broadcast_to(cs[:, ts - 1:ts], carry_ref.shape)


def _pre_attn(x, sc, sh, g_pre, w1, wfl, bfg, g_q, wqa, wqb, g_kv, wuk, wuv, cp, sp, ts):
    bsz, s, d = x.shape
    grid = (bsz, s // ts)
    tok = lambda width: pl.BlockSpec((1, ts, width), lambda b, i: (b, i, 0))
    per_b = pl.BlockSpec((1, 1, d), lambda b, i: (b, 0, 0))
    full = lambda a: pl.BlockSpec(a.shape, lambda b, i: (0,) * a.ndim)
    outs = [
        jax.ShapeDtypeStruct((bsz, s, MLA_HEADS * LANES), BF16),
        jax.ShapeDtypeStruct((bsz, s, MLA_HEADS * LANES), BF16),
        jax.ShapeDtypeStruct((bsz, s, MLA_HEADS * V_DIM), BF16),
        jax.ShapeDtypeStruct((bsz, s, FOX_WIDTH), BF16),
        jax.ShapeDtypeStruct((bsz, s, FOX_WIDTH), BF16),
        jax.ShapeDtypeStruct((bsz, s, FOX_WIDTH), BF16),
        jax.ShapeDtypeStruct((bsz, FL_ROWS, s), F32),
    ]
    out_specs = [tok(MLA_HEADS * LANES), tok(MLA_HEADS * LANES), tok(MLA_HEADS * V_DIM),
                 tok(FOX_WIDTH), tok(FOX_WIDTH), tok(FOX_WIDTH),
                 pl.BlockSpec((1, FL_ROWS, ts), lambda b, i: (b, 0, i))]
    return pl.pallas_call(
        _pre_attn_kernel,
        grid=grid,
        in_specs=[tok(d), per_b, per_b, full(g_pre), full(w1), full(wfl), full(bfg), full(g_q),
                  full(wqa), full(wqb), full(g_kv), full(wuk), full(wuv), tok(LANES), tok(LANES)],
        out_specs=out_specs,
        out_shape=outs,
        scratch_shapes=[pltpu.VMEM((FL_ROWS, LANES), F32)],
        compiler_params=_cparams(("arbitrary", "arbitrary")),
        name="pre_attn",
    )(x, sc, sh, g_pre, w1, wfl, bfg, g_q, wqa, wqb, g_kv, wuk, wuv, cp, sp)


def _attn_kernel(*refs, tq, head_lanes, has_bias):
    if has_bias:
        q_ref, k_ref, v_ref, f_ref, o_ref, acc_ref, m_ref, l_ref = refs
    else:
        q_ref, k_ref, v_ref, o_ref, acc_ref, m_ref, l_ref = refs
        f_ref = None
    qi = pl.program_id(2)
    n_heads = acc_ref.shape[0]
    lane = lax.broadcasted_iota(jnp.int32, (tq, LANES), 1)
    qs = []
    for hd in range(n_heads):
        if head_lanes == LANES:
            qs.append(q_ref[0, :, hd * LANES:(hd + 1) * LANES])
        else:
            q2 = q_ref[0, :, (hd // 2) * LANES:(hd // 2 + 1) * LANES]
            keep = (lane < head_lanes) if hd % 2 == 0 else (lane >= head_lanes)
            qs.append(jnp.where(keep, q2, jnp.zeros_like(q2)))

    def step(off, width, masked):
        scores = []
        for hd in range(n_heads):
            pr = hd // 2
            if head_lanes == LANES:
                kk = k_ref[0, pl.ds(off, width), hd * LANES:(hd + 1) * LANES]
            else:
                kk = k_ref[0, pl.ds(off, width), pr * LANES:(pr + 1) * LANES]
            scores.append(_dot_nt(qs[hd], kk))
        for hd in range(n_heads):
            pr = hd // 2
            v2 = v_ref[0, pl.ds(off, width), pr * LANES:(pr + 1) * LANES]
            s = scores[hd]
            if has_bias:
                s = s - f_ref[0, pr, hd % 2:hd % 2 + 1, pl.ds(off, width)]
            if masked:
                rr = lax.broadcasted_iota(jnp.int32, (tq, width), 0)
                cc = lax.broadcasted_iota(jnp.int32, (tq, width), 1)
                s = jnp.where(cc <= rr, s, NEG_INF)
                m_new = jnp.broadcast_to(jnp.max(s, axis=-1, keepdims=True), (tq, LANES))
                p = jnp.exp2(s - jnp.tile(m_new, (1, width // LANES)))
                l_ref[hd] = jnp.broadcast_to(jnp.sum(p, axis=-1, keepdims=True), (tq, LANES))
                m_ref[hd] = m_new
                acc_ref[hd] = _dot(p.astype(BF16), v2)
            else:
                m = m_ref[hd]
                m_new = jnp.maximum(m, jnp.max(s, axis=-1, keepdims=True))
                alpha = jnp.exp2(m - m_new)
                p = jnp.exp2(s - jnp.tile(m_new, (1, width // LANES)))
                l_ref[hd] = alpha * l_ref[hd] + jnp.sum(p, axis=-1, keepdims=True)
                m_ref[hd] = m_new
                acc_ref[hd] = alpha * acc_ref[hd] + _dot(p.astype(BF16), v2)

    step(pl.multiple_of(qi * tq, tq), tq, True)

    per_wide = ATTN_WIDE // tq

    def wide(j, _):
        step(pl.multiple_of(j * ATTN_WIDE, ATTN_WIDE), ATTN_WIDE, False)
        return 0

    def single(j, _):
        step(pl.multiple_of(((qi // per_wide) * per_wide + j) * tq, tq), tq, False)
        return 0

    lax.fori_loop(0, qi // per_wide, wide, 0)
    lax.fori_loop(0, qi % per_wide, single, 0)

    for pr in range(n_heads // 2):
        o0 = acc_ref[2 * pr] / l_ref[2 * pr]
        o1 = acc_ref[2 * pr + 1] / l_ref[2 * pr + 1]
        o_ref[0, :, pr * LANES:(pr + 1) * LANES] = jnp.where(lane < V_DIM, o0, o1).astype(o_ref.dtype)


ATTN_PAIRS = 2
ATTN_WIDE = 1024


def _attention(q, k, v, fcum, tq, head_lanes):
    bsz, s, _ = q.shape
    n_groups = v.shape[2] // (ATTN_PAIRS * LANES)
    qk_w = ATTN_PAIRS * 2 * head_lanes
    v_w = ATTN_PAIRS * LANES
    n_heads = 2 * ATTN_PAIRS
    has_bias = fcum is not None
    in_specs = [
        pl.BlockSpec((1, tq, qk_w), lambda b, p, i: (b, i, p)),
        pl.BlockSpec((1, s, qk_w), lambda b, p, i: (b, 0, p)),
        pl.BlockSpec((1, s, v_w), lambda b, p, i: (b, 0, p)),
    ]
    args = [q, k, v]
    if has_bias:
        in_specs.append(pl.BlockSpec((1, ATTN_PAIRS, 2, s), lambda b, p, i: (b, p, 0, 0)))
        args.append(fcum)
    return pl.pallas_call(
        functools.partial(_attn_kernel, tq=tq, head_lanes=head_lanes, has_bias=has_bias),
        grid=(bsz, n_groups, s // tq),
        in_specs=in_specs,
        out_specs=pl.BlockSpec((1, tq, v_w), lambda b, p, i: (b, i, p)),
        out_shape=jax.ShapeDtypeStruct((bsz, s, v.shape[2]), BF16),
        scratch_shapes=[pltpu.VMEM((n_heads, tq, LANES), F32), pltpu.VMEM((n_heads, tq, LANES), F32),
                        pltpu.VMEM((n_heads, tq, LANES), F32)],
        compiler_params=_cparams(("arbitrary", "arbitrary", "arbitrary")),
        name="attn_fox" if has_bias else "attn_mla",
    )(*args)


def _pack_bf16_pairs(xb):
    w = xb.shape[1] // 2
    lo = lax.bitcast_convert_type(xb[:, :w].astype(F32), jnp.uint32)
    hi = lax.bitcast_convert_type(xb[:, w:].astype(F32), jnp.uint32)
    return (hi & jnp.uint32(0xFFFF0000)) | lax.shift_right_logical(lo, jnp.uint32(16))


def _unpack_bf16_pairs(xw):
    lo = lax.bitcast_convert_type(lax.shift_left(xw, jnp.uint32(16)), F32)
    hi = lax.bitcast_convert_type(xw & jnp.uint32(0xFFFF0000), F32)
    return jnp.concatenate([lo, hi], axis=1).astype(BF16)


def _post_attn_kernel(om_ref, of_ref, x_ref, wo_ref, gpost_ref, gate_ref, gmoe_ref, sc_ref, sh_ref,
                      wr_ref, br_ref, x1_ref, h2_ref, idx_ref, gts_ref):
    half = om_ref.shape[2]
    o = _dot(om_ref[0], wo_ref[0:half, :]) + _dot(of_ref[0], wo_ref[half:, :])
    x1 = x_ref[0] + gate_ref[0] * _rms(o, gpost_ref[...])
    x1_ref[0] = x1
    h2 = _rms(x1, gmoe_ref[...]) * (1.0 + sc_ref[0]) + sh_ref[0]
    hb = h2.astype(BF16)
    h2_ref[0] = _pack_bf16_pairs(hb)
    logits = _dot_nt(wr_ref[...], hb) + br_ref[...]
    n_e = logits.shape[0]
    eid = lax.broadcasted_iota(jnp.int32, logits.shape, 0)
    vals, idxs = [], []
    for _ in range(TOP_K):
        m = jnp.max(logits, axis=0, keepdims=True)
        ix = jnp.min(jnp.where(logits == m, eid, n_e), axis=0, keepdims=True)
        vals.append(m)
        idxs.append(ix)
        logits = jnp.where(eid == ix, -jnp.inf, logits)
    es = [jnp.exp(vv - vals[0]) for vv in vals]
    den = es[0] + es[1] + es[2] + es[3]
    for kk in range(TOP_K):
        idx_ref[0, kk:kk + 1, :] = idxs[kk]
        gts_ref[0, kk:kk + 1, :] = es[kk] / den


def _post_attn(o_mla, o_fox, x, wo, g_post, gate, g_moe, sc, sh, wr_t, br, ts):
    bsz, s, d = x.shape
    half = o_mla.shape[2]
    tok = lambda width: pl.BlockSpec((1, ts, width), lambda b, i: (b, i, 0))
    per_b = pl.BlockSpec((1, 1, d), lambda b, i: (b, 0, 0))
    full = lambda a: pl.BlockSpec(a.shape, lambda b, i: (0,) * a.ndim)
    k_spec = pl.BlockSpec((1, TOP_K, ts), lambda b, i: (b, 0, i))
    return pl.pallas_call(
        _post_attn_kernel,
        grid=(bsz, s // ts),
        in_specs=[tok(half), tok(half), tok(d), full(wo), full(g_post), per_b, full(g_moe), per_b,
                  per_b, full(wr_t), full(br)],
        out_specs=[tok(d), tok(d // 2), k_spec, k_spec],
        out_shape=[jax.ShapeDtypeStruct((bsz, s, d), F32),
                   jax.ShapeDtypeStruct((bsz, s, d // 2), jnp.uint32),
                   jax.ShapeDtypeStruct((bsz, TOP_K, s), jnp.int32),
                   jax.ShapeDtypeStruct((bsz, TOP_K, s), F32)],
        compiler_params=_cparams(("arbitrary", "arbitrary")),
        name="post_attn",
    )(o_mla, o_fox, x, wo, g_post, gate, g_moe, sc, sh, wr_t, br)


def _route_kernel(idx_ref, tri_ref, rank_ref, cnt_ref, run_ref):
    first = jnp.logical_and(pl.program_id(0) == 0, pl.program_id(1) == 0)

    @pl.when(first)
    def _():
        run_ref[...] = jnp.zeros_like(run_ref)

    tr = idx_ref.shape[2]
    eid = lax.broadcasted_iota(jnp.int32, (N_EXPERTS, tr), 0)
    hot = [eid == idx_ref[0, kk:kk + 1, :] for kk in range(TOP_K)]
    multi = hot[0] | hot[1] | hot[2] | hot[3]
    mf = jnp.where(multi, 1.0, 0.0)
    incl = _dot(mf.astype(BF16), tri_ref[...]) + run_ref[:, 0:1]
    excl = incl - mf
    for kk in range(TOP_K):
        rank_ref[0, kk:kk + 1, :] = jnp.sum(jnp.where(hot[kk], excl, 0.0), axis=0,
                                            keepdims=True).astype(jnp.int32)
    run_ref[...] = jnp.broadcast_to(incl[:, tr - 1:tr], run_ref.shape)
    cnt_ref[...] = run_ref[...].astype(jnp.int32)


def _route(idx_kt, tr):
    bsz, _, s = idx_kt.shape
    tri = (jnp.arange(tr)[:, None] <= jnp.arange(tr)[None, :]).astype(BF16)
    k_spec = pl.BlockSpec((1, TOP_K, tr), lambda b, i: (b, 0, i))
    return pl.pallas_call(
        _route_kernel,
        grid=(bsz, s // tr),
        in_specs=[k_spec, pl.BlockSpec((tr, tr), lambda b, i: (0, 0))],
        out_specs=[k_spec, pl.BlockSpec((N_EXPERTS, LANES), lambda b, i: (0, 0))],
        out_shape=[jax.ShapeDtypeStruct((bsz, TOP_K, s), jnp.int32),
                   jax.ShapeDtypeStruct((N_EXPERTS, LANES), jnp.int32)],
        scratch_shapes=[pltpu.VMEM((N_EXPERTS, LANES), F32)],
        compiler_params=_cparams(("arbitrary", "arbitrary")),
        name="route",
    )(idx_kt, tri)


SC_CORES = 2
SC_SUBCORES = 16
SC_CHUNK = 64
MOE_GROUP_FRACTIONS = (1 / 18, 1 / 6, 1 / 3)


def _sc_gather(table, idx):
    m = idx.shape[0]
    width = table.shape[1]
    n_workers = SC_CORES * SC_SUBCORES
    per_worker = m // n_workers
    n_chunks = per_worker // SC_CHUNK
    assert n_chunks * SC_CHUNK * n_workers == m
    n_buf = next(nb for nb in (3, 2, 1) if n_chunks % nb == 0)
    mesh = plsc.VectorSubcoreMesh(core_axis_name="c", subcore_axis_name="s")

    @functools.partial(
        pl.kernel, mesh=mesh,
        out_type=jax.ShapeDtypeStruct((m, width), table.dtype),
        scratch_types=[pltpu.VMEM((n_chunks, SC_CHUNK), jnp.int32),
                       pltpu.VMEM((n_buf, SC_CHUNK, width), table.dtype),
                       pltpu.SemaphoreType.DMA((n_buf,)),
                       pltpu.SemaphoreType.DMA((n_buf,))],
        name="sc_gather",
    )
    def gather(table_hbm, idx_hbm, out_hbm, idx_v, rows_v, sem_g, sem_w):
        wid = lax.axis_index("s") * SC_CORES + lax.axis_index("c")
        base = wid * per_worker
        pltpu.sync_copy(idx_hbm.at[wid], idx_v)

        def gather_copy(j, b):
            return pltpu.make_async_copy(table_hbm.at[idx_v.at[j]], rows_v.at[b], sem_g.at[b])

        for b in range(n_buf):
            gather_copy(b, b).start()

        @pl.loop(0, n_chunks, step=n_buf)
        def _(g):
            for b in range(n_buf):
                j = g + b
                gather_copy(j, b).wait()
                off = pl.multiple_of(base + j * SC_CHUNK, SC_CHUNK)
                write = pltpu.make_async_copy(rows_v.at[b], out_hbm.at[pl.ds(off, SC_CHUNK)],
                                              sem_w.at[b])
                write.start()
                write.wait()

                @pl.when(j + n_buf < n_chunks)
                def _():
                    gather_copy(j + n_buf, b).start()

    return gather(table, idx.reshape(n_workers, n_chunks, SC_CHUNK))


def _expert_mlp(xb, wgu_bf, bgu_ref, wd_bf, bd_ref):
    ff = wd_bf.shape[0]
    gu = _dot(xb, wgu_bf[...]) + bgu_ref[0]
    g = jnp.minimum(gu[:, :ff], SWIGLU_LIMIT)
    u = jnp.clip(gu[:, ff:], -SWIGLU_LIMIT, SWIGLU_LIMIT)
    glu = g * jax.nn.sigmoid(SWIGLU_ALPHA * g)
    act = ((u + 1.0) * glu).astype(BF16)
    return _dot(act, wd_bf[...]) + bd_ref[0]


def _expert_pregathered_kernel(be_ref, first_ref, n_ref, sidx_ref, x_ref, wgu_ref, bgu_ref, wd_ref,
                               bd_ref, *rest):
    y_hbm, ybuf0, ybuf1, wgu_bf, wd_bf, sem_s = rest[-6:]
    i = pl.program_id(0)
    n = n_ref[0]
    bm = ybuf0.shape[0]
    ybufs = (ybuf0, ybuf1)

    def scatter_start(slot, r, priority=0):
        pltpu.make_async_copy(ybufs[slot].at[pl.ds(r, 1), :],
                              y_hbm.at[pl.ds(sidx_ref[0, 0, r], 1), :],
                              sem_s.at[slot]).start(priority)

    def scatter_wait(slot):
        pltpu.make_async_copy(ybufs[slot], y_hbm.at[pl.ds(0, bm), :], sem_s.at[slot]).wait()

    @pl.when(jnp.logical_and(first_ref[i] == 1, i < n))
    def _():
        wgu_bf[...] = wgu_ref[0].astype(BF16)
        wd_bf[...] = wd_ref[0].astype(BF16)

    def step(par):
        if par == 0:
            @pl.when(i == 0)
            def _():
                ybuf1[...] = jnp.zeros_like(ybuf1)

        @pl.when(jnp.logical_and(i >= 1, i <= n))
        def _():
            scatter_wait(par)

        @pl.when(i < n)
        def _():
            xb = _unpack_bf16_pairs(x_ref[...])
            for r in range(bm):
                scatter_start(1 - par, r, r % 2)
            ybufs[par][...] = _expert_mlp(xb, wgu_bf, bgu_ref, wd_bf, bd_ref)

        @pl.when(i == n)
        def _():
            def body(r, _):
                scatter_start(1 - par, r)
                return 0
            lax.fori_loop(0, bm, body, 0, unroll=8)
            scatter_wait(1 - par)

    @pl.when(i % 2 == 0)
    def _():
        step(0)

    @pl.when(i % 2 == 1)
    def _():
        step(1)


def _experts_pregathered(be_step, first_step, n_real, sidx, xs, w_gu, b_gu, w_d, b_d, y4, n_out_rows,
                         bm):
    e, d, f2 = w_gu.shape
    ff = w_d.shape[1]
    n_steps = sidx.shape[0]
    n_xblocks = xs.shape[0] // bm
    n_prefetch = 3
    prev = [] if y4 is None else [y4]
    return pl.pallas_call(
        _expert_pregathered_kernel,
        grid_spec=pltpu.PrefetchScalarGridSpec(
            num_scalar_prefetch=n_prefetch,
            grid=(n_steps,),
            in_specs=[
                pl.BlockSpec((1, 1, bm), lambda i, be, fi, nu: (i, 0, 0), memory_space=pltpu.SMEM),
                pl.BlockSpec((bm, d // 2), lambda i, be, fi, nu: (jnp.minimum(i, n_xblocks - 1), 0)),
                pl.BlockSpec((1, d, f2), lambda i, be, fi, nu: (be[i], 0, 0)),
                pl.BlockSpec((1, 1, f2), lambda i, be, fi, nu: (be[i], 0, 0)),
                pl.BlockSpec((1, ff, d), lambda i, be, fi, nu: (be[i], 0, 0)),
                pl.BlockSpec((1, 1, d), lambda i, be, fi, nu: (be[i], 0, 0)),
            ] + [pl.BlockSpec(memory_space=pl.ANY) for _ in prev],
            out_specs=pl.BlockSpec(memory_space=pl.ANY),
            scratch_shapes=[pltpu.VMEM((bm, d), F32), pltpu.VMEM((bm, d), F32),
                            pltpu.VMEM((d, f2), BF16), pltpu.VMEM((ff, d), BF16),
                            pltpu.SemaphoreType.DMA((2,))],
        ),
        out_shape=jax.ShapeDtypeStruct((n_out_rows, d), F32),
        input_output_aliases={n_prefetch + 6: 0} if prev else {},
        compiler_params=_cparams(("arbitrary",)),
        name="experts",
    )(be_step, first_step, n_real, sidx, xs, w_gu, b_gu.reshape(e, 1, f2), w_d, b_d.reshape(e, 1, d),
      *prev)


def _combine_kernel(y0_ref, y1_ref, y2_ref, y3_ref, gts_ref, x1_ref, gpost_ref, gate_ref, o_ref):
    gts = gts_ref[...]
    acc = gts[:, 0:1] * y0_ref[...]
    for kk, y_ref in enumerate((y1_ref, y2_ref, y3_ref), start=1):
        acc = acc + gts[:, kk:kk + 1] * y_ref[...]
    o_ref[0] = x1_ref[0] + gate_ref[0] * _rms(acc, gpost_ref[...])


def _combine(y4, gts, x1, g_post, gate, tt):
    bsz, s, d = x1.shape
    n_t = s // tt
    n_tiles = bsz * n_t
    y_spec = lambda kk: pl.BlockSpec((tt, d), lambda b, i: (kk * n_tiles + b * n_t + i, 0))
    return pl.pallas_call(
        _combine_kernel,
        grid=(bsz, n_t),
        in_specs=[
            y_spec(0), y_spec(1), y_spec(2), y_spec(3),
            pl.BlockSpec((tt, TOP_K), lambda b, i: (b * n_t + i, 0)),
            pl.BlockSpec((1, tt, d), lambda b, i: (b, i, 0)),
            pl.BlockSpec(g_post.shape, lambda b, i: (0, 0)),
            pl.BlockSpec((1, 1, d), lambda b, i: (b, 0, 0)),
        ],
        out_specs=pl.BlockSpec((1, tt, d), lambda b, i: (b, i, 0)),
        out_shape=jax.ShapeDtypeStruct((bsz, s, d), F32),
        compiler_params=_cparams(("arbitrary", "arbitrary")),
        name="combine",
    )(y4, y4, y4, y4, gts, x1, g_post, gate)


def _rope_patterns(positions):
    inv_freq = ROPE_THETA ** (-jnp.arange(0, ROPE_DIM, 2, dtype=F32) / ROPE_DIM)
    hr = ROPE_DIM // 2
    lane = jnp.arange(LANES)
    lo_half = (lane >= NOPE_DIM) & (lane < NOPE_DIM + hr)
    hi_half = (lane >= NOPE_DIM + hr) & (lane < NOPE_DIM + ROPE_DIM)
    freq = jnp.zeros((LANES,), F32).at[NOPE_DIM:NOPE_DIM + hr].set(inv_freq)
    freq = freq.at[NOPE_DIM + hr:NOPE_DIM + ROPE_DIM].set(inv_freq)
    ang = positions.astype(F32)[..., None] * freq
    cp = jnp.where(lane < NOPE_DIM, 1.0, jnp.where(lo_half | hi_half, jnp.cos(ang), 0.0))
    sp = jnp.sin(ang) * jnp.where(lo_half, -1.0, jnp.where(hi_half, 1.0, 0.0))
    return cp.astype(F32), sp.astype(F32)


def _prep_mixer_weights(w_in, w_uq, w_ukv, b_forget):
    d = w_in.shape[0]
    o_kr = Q_RANK + KV_RANK
    o_f = o_kr + ROPE_DIM
    hr = ROPE_DIM // 2
    z = lambda n: jnp.zeros((d, n), w_in.dtype)
    kr = w_in[:, o_kr:o_f]
    kr_sw = jnp.concatenate([kr[:, hr:], kr[:, :hr]], axis=1)
    pad = LANES - NOPE_DIM - ROPE_DIM
    w1 = jnp.concatenate([
        w_in[:, :o_kr],
        z(NOPE_DIM), kr, z(pad),
        z(NOPE_DIM), kr_sw, z(pad),
        w_in[:, o_f:o_f + 3 * FOX_WIDTH],
    ], axis=1).astype(BF16)
    wfl = jnp.zeros((FL_ROWS, d), w_in.dtype).at[:FOX_HEADS].set(
        w_in[:, o_f + 3 * FOX_WIDTH:].T).astype(BF16)
    bfg = jnp.zeros((FL_ROWS, 1), F32).at[:FOX_HEADS, 0].set(b_forget)

    wq = w_uq.reshape(Q_RANK, MLA_HEADS, NOPE_DIM + ROPE_DIM)
    nope, rope = wq[..., :NOPE_DIM], wq[..., NOPE_DIM:]
    zq = lambda n: jnp.zeros((Q_RANK, MLA_HEADS, n), w_uq.dtype)
    wqa = jnp.concatenate([nope, rope, zq(pad)], axis=-1).reshape(Q_RANK, -1).astype(BF16)
    wqb = jnp.concatenate([zq(NOPE_DIM), rope[..., hr:], rope[..., :hr], zq(pad)],
                          axis=-1).reshape(Q_RANK, -1).astype(BF16)
    wkv = w_ukv.reshape(KV_RANK, MLA_HEADS, NOPE_DIM + V_DIM)
    wuk = jnp.concatenate([wkv[..., :NOPE_DIM],
                           jnp.zeros((KV_RANK, MLA_HEADS, LANES - NOPE_DIM), w_ukv.dtype)],
                          axis=-1).reshape(KV_RANK, -1).astype(BF16)
    wuv = wkv[..., NOPE_DIM:].reshape(KV_RANK, -1).astype(BF16)
    return w1, wfl, bfg, wqa, wqb, wuk, wuv


def _slot_tables(idx_kt, rank_kt, counts, bm):
    bsz, _, s = idx_kt.shape
    t = bsz * s
    n_blocks = (t * TOP_K) // bm + N_EXPERTS
    n_real = t * TOP_K
    i32 = jnp.int32
    eids = jnp.arange(N_EXPERTS, dtype=i32)

    pcounts = ((counts + bm - 1) // bm) * bm
    incl_mat = (eids[:, None] <= eids[None, :]).astype(i32)
    pends = pcounts @ incl_mat
    pstarts = pends - pcounts
    cstarts = counts @ incl_mat - counts
    total = pends[-1]
    n_used = (total // bm).astype(i32).reshape(1)

    onehot = idx_kt[..., None] == eids
    dest = jnp.sum(jnp.where(onehot, pstarts, 0), axis=-1) + rank_kt
    tok = jnp.arange(t, dtype=i32).reshape(bsz, 1, s)
    kk = jnp.arange(TOP_K, dtype=i32).reshape(1, TOP_K, 1)
    yrow_real = jnp.broadcast_to(kk * t + tok, dest.shape)
    _, rows_sorted = lax.sort((dest.reshape(-1), yrow_real.reshape(-1)), num_keys=1)

    blk0 = jnp.arange(n_blocks, dtype=i32)
    be_blk = jnp.minimum(jnp.sum((pends[None, :] <= (blk0 * bm)[:, None]).astype(i32), axis=1),
                         N_EXPERTS - 1)
    hot_b = be_blk[:, None] == eids[None, :]
    pick = lambda v: jnp.sum(jnp.where(hot_b, v[None, :], 0), axis=1)[:, None]
    local = (blk0 * bm)[:, None] + jnp.arange(bm, dtype=i32)[None, :] - pick(pstarts)
    valid = jnp.logical_and(local < pick(counts), (blk0 < n_used[0])[:, None])
    j = jnp.clip(pick(cstarts) + local, 0, n_real - 1)
    rows = jnp.take(rows_sorted, j.reshape(-1), axis=0).reshape(n_blocks, bm)
    spare = TOP_K * t + jnp.arange(bm, dtype=i32)[None, :]
    yrow = jnp.where(valid, rows, spare)
    slot = (blk0 * bm)[:, None] + jnp.arange(bm, dtype=i32)[None, :]
    src_tok = jnp.where(valid, rows % t, slot % t)

    return src_tok, yrow, be_blk, n_used[0]


def _first_flags(be_step):
    return jnp.concatenate([jnp.ones((1,), jnp.int32),
                            (be_step[1:] != be_step[:-1]).astype(jnp.int32)])


def _group_sizes(n_blocks, bm):
    unit = 3 * (SC_CORES * SC_SUBCORES * SC_CHUNK) // bm
    sizes = [max(unit, (int(n_blocks * f) // unit) * unit) for f in MOE_GROUP_FRACTIONS]
    sizes.append(n_blocks - sum(sizes))
    assert sizes[-1] > 0 and sizes[-1] % unit == 0
    return sizes


def _moe_blocks(src_tok, yrow, be_blk, n_used, h2p, w_gu, b_gu, w_d, b_d, n_out_rows, bm):
    i32 = jnp.int32
    n_blocks = src_tok.shape[0]
    spare = (n_out_rows - bm) + jnp.arange(bm, dtype=i32)[None, :]
    y4 = None
    lo = 0
    for nb in _group_sizes(n_blocks, bm):
        xs = _sc_gather(h2p, src_tok[lo:lo + nb].reshape(-1))
        sidx = jnp.concatenate([spare, yrow[lo:lo + nb]]).reshape(nb + 1, 1, bm)
        be = be_blk[lo + jnp.minimum(jnp.arange(nb + 1, dtype=i32), nb - 1)]
        n_g = jnp.clip(n_used - lo, 0, nb).astype(i32).reshape(1)
        y4 = _experts_pregathered(be, _first_flags(be), n_g, sidx, xs, w_gu, b_gu, w_d, b_d, y4,
                                  n_out_rows, bm)
        lo += nb
    return y4


def _layer(x, ada, cp, sp, g_attn_pre, g_attn_post, w_in, g_q_norm, w_uq, g_kv_norm, w_ukv,
           b_forget, w_out, g_moe_pre, g_moe_post, w_router, b_router, w_gate_up, b_gate_up,
           w_down, b_down):
    bsz, s, d = x.shape
    t = bsz * s
    ts = min(512, s)
    tq = min(512, s)
    tt = min(512, s)
    tr = min(1024, s)
    bm = 512
    row = lambda v: v.reshape(1, -1)
    sh_a, sc_a, gt_a, sh_m, sc_m, gt_m = [v.reshape(bsz, 1, d) for v in jnp.split(ada, 6, axis=-1)]

    w1, wfl, bfg, wqa, wqb, wuk, wuv = _prep_mixer_weights(w_in, w_uq, w_ukv, b_forget)
    q, k, v, fq, fk, fv, fcum = _pre_attn(x, sc_a, sh_a, row(g_attn_pre), w1, wfl, bfg,
                                          row(g_q_norm), wqa, wqb, row(g_kv_norm), wuk, wuv,
                                          cp, sp, ts)
    o_mla = _attention(q, k, v, None, tq, LANES)
    fpairs = fcum[:, :FOX_HEADS].reshape(bsz, FOX_HEADS // 2, 2, s)
    o_fox = _attention(fq, fk, fv, fpairs, tq, FOX_DIM)

    x1, h2, idx_kt, gts_kt = _post_attn(o_mla, o_fox, x, w_out.astype(BF16), row(g_attn_post), gt_a,
                                        row(g_moe_pre), sc_m, sh_m, w_router.T.astype(BF16),
                                        b_router.reshape(-1, 1), ts)

    rank_kt, cnt = _route(idx_kt, tr)
    src_tok, yrow, be_blk, n_used = _slot_tables(idx_kt, rank_kt, cnt[:, 0], bm)
    y4 = _moe_blocks(src_tok, yrow, be_blk, n_used, h2.reshape(t, d // 2), w_gate_up, b_gate_up,
                     w_down, b_down, TOP_K * t + bm, bm)
    gts = gts_kt.transpose(0, 2, 1).reshape(t, TOP_K)
    return _combine(y4, gts, x1, row(g_moe_post), gt_m, tt)


def kernel(x, c, positions, w_ada, b_ada, g_attn_pre, g_attn_post, w_in, g_q_norm, w_uq, g_kv_norm,
           w_ukv, b_forget, w_out, g_moe_pre, g_moe_post, w_router, b_router, w_gate_up, b_gate_up,
           w_down, b_down):
    cp, sp = _rope_patterns(positions)
    for layer in range(w_ada.shape[0]):
        ada = _ada(c, w_ada[layer], b_ada[layer])
        x = _layer(x, ada, cp, sp, g_attn_pre[layer], g_attn_post[layer], w_in[layer],
                   g_q_norm[layer], w_uq[layer], g_kv_norm[layer], w_ukv[layer], b_forget[layer],
                   w_out[layer], g_moe_pre[layer], g_moe_post[layer], w_router[layer],
                   b_router[layer], w_gate_up[layer], b_gate_up[layer], w_down[layer],
                   b_down[layer])
    return x
```

```python
import functools
import math

import jax
import jax.numpy as jnp
from jax import lax
from jax.experimental import pallas as pl
from jax.experimental.pallas import tpu as pltpu
from jax.experimental.pallas import tpu_sc as plsc

F32 = jnp.float32
BF16 = jnp.bfloat16

MLA_HEADS = 8
NOPE_DIM = 64
ROPE_DIM = 32
V_DIM = 64
Q_RANK = 256
KV_RANK = 128
FOX_HEADS = 8
FOX_DIM = 64
FOX_WIDTH = FOX_HEADS * FOX_DIM
ROPE_THETA = 10000.0
N_EXPERTS = 32
TOP_K = 4
SWIGLU_ALPHA = 1.702
SWIGLU_LIMIT = 7.0
RMS_EPS = 1e-6
NEG_INF = -1e30
LOG2E = math.log2(math.e)
MLA_SCALE = LOG2E / math.sqrt(NOPE_DIM + ROPE_DIM)
FOX_SCALE = LOG2E / math.sqrt(FOX_DIM)

LANES = 128
C_CQ = (0, 256)
C_CKV = (256, 384)
C_KR = (384, 640)
C_FQ = (640, 1152)
C_FK = (1152, 1664)
C_FV = (1664, 2176)
FL_ROWS = 16

VMEM_LIMIT = 56 * 1024 * 1024


def _cparams(sem):
    return pltpu.CompilerParams(dimension_semantics=sem, vmem_limit_bytes=VMEM_LIMIT)


def _rms(x, g):
    return x * lax.rsqrt(jnp.mean(x * x, axis=-1, keepdims=True) + RMS_EPS) * g


def _dot(a, b):
    return jnp.dot(a, b, preferred_element_type=F32)


def _dot_nt(a, b):
    return lax.dot_general(a, b, (((1,), (1,)), ((), ())), preferred_element_type=F32)


def _ada_kernel(c_ref, w_ref, b_ref, o_ref):
    c = c_ref[...]
    ca = (c * jax.nn.sigmoid(c)).astype(BF16)
    o_ref[...] = _dot(ca, w_ref[...].astype(BF16)) + b_ref[...]


def _ada(c, w, b):
    bsz, d = c.shape
    n = w.shape[1]
    bn = 1024
    return pl.pallas_call(
        _ada_kernel,
        grid=(n // bn,),
        in_specs=[
            pl.BlockSpec((bsz, d), lambda j: (0, 0)),
            pl.BlockSpec((d, bn), lambda j: (0, j)),
            pl.BlockSpec((1, bn), lambda j: (0, j)),
        ],
        out_specs=pl.BlockSpec((bsz, bn), lambda j: (0, j)),
        out_shape=jax.ShapeDtypeStruct((bsz, n), F32),
        compiler_params=_cparams(("arbitrary",)),
        name="ada",
    )(c, w, b.reshape(1, n))


def _pre_attn_kernel(x_ref, sc_ref, sh_ref, gpre_ref, w1_ref, wfl_ref, bf_ref, gq_ref, wqa_ref,
                     wqb_ref, gkv_ref, wuk_ref, wuv_ref, cp_ref, sp_ref,
                     q_ref, k_ref, v_ref, fq_ref, fk_ref, fv_ref, fc_ref, carry_ref):
    si = pl.program_id(1)
    ts = x_ref.shape[1]
    x = x_ref[0]
    h = _rms(x, gpre_ref[...]) * (1.0 + sc_ref[0]) + sh_ref[0]
    hb = h.astype(BF16)

    def proj(c):
        return _dot(hb, w1_ref[:, c[0]:c[1]])

    cp = cp_ref[0]
    sp = sp_ref[0]

    cqn = _rms(proj(C_CQ), gq_ref[...]).astype(BF16)
    qa = _dot(cqn, wqa_ref[...])
    qb = _dot(cqn, wqb_ref[...])
    for hd in range(MLA_HEADS):
        sl = slice(hd * LANES, (hd + 1) * LANES)
        q_ref[0, :, sl] = ((qa[:, sl] * cp + qb[:, sl] * sp) * MLA_SCALE).astype(BF16)

    kr = proj(C_KR)
    k_rope = kr[:, :LANES] * cp + kr[:, LANES:] * sp
    ckvn = _rms(proj(C_CKV), gkv_ref[...]).astype(BF16)
    kn = _dot(ckvn, wuk_ref[...])
    for hd in range(MLA_HEADS):
        sl = slice(hd * LANES, (hd + 1) * LANES)
        k_ref[0, :, sl] = (kn[:, sl] + k_rope).astype(BF16)
    v_ref[0] = _dot(ckvn, wuv_ref[...]).astype(BF16)

    fq_ref[0] = (proj(C_FQ) * FOX_SCALE).astype(BF16)
    fk_ref[0] = proj(C_FK).astype(BF16)
    fv_ref[0] = proj(C_FV).astype(BF16)

    fl = _dot_nt(wfl_ref[...], hb) + bf_ref[...]
    lf = jnp.minimum(fl, 0.0) - jnp.log1p(jnp.exp(-jnp.abs(fl)))
    r = lax.broadcasted_iota(jnp.int32, (ts, ts), 0)
    c = lax.broadcasted_iota(jnp.int32, (ts, ts), 1)
    tri = (r <= c).astype(BF16)
    p0 = lf.astype(BF16)
    r1 = lf - p0.astype(F32)
    p1 = r1.astype(BF16)
    p2 = (r1 - p1.astype(F32)).astype(BF16)
    cs = _dot(p0, tri) + _dot(p1, tri) + _dot(p2, tri)

    @pl.when(si == 0)
    def _():
        carry_ref[...] = jnp.zeros_like(carry_ref)

    cs = cs + carry_ref[:, 0:1]
    fc_ref[0] = cs * LOG2E
    carry_ref[...] = jnp.broadcast_to(cs[:, ts - 1:ts], carry_ref.shape)


def _pre_attn(x, sc, sh, g_pre, w1, wfl, bfg, g_q, wqa, wqb, g_kv, wuk, wuv, cp, sp, ts):
    bsz, s, d = x.shape
    grid = (bsz, s // ts)
    tok = lambda width: pl.BlockSpec((1, ts, width), lambda b, i: (b, i, 0))
    per_b = pl.BlockSpec((1, 1, d), lambda b, i: (b, 0, 0))
    full = lambda a: pl.BlockSpec(a.shape, lambda b, i: (0,) * a.ndim)
    outs = [
        jax.ShapeDtypeStruct((bsz, s, MLA_HEADS * LANES), BF16),
        jax.ShapeDtypeStruct((bsz, s, MLA_HEADS * LANES), BF16),
        jax.ShapeDtypeStruct((bsz, s, MLA_HEADS * V_DIM), BF16),
        jax.ShapeDtypeStruct((bsz, s, FOX_WIDTH), BF16),
        jax.ShapeDtypeStruct((bsz, s, FOX_WIDTH), BF16),
        jax.ShapeDtypeStruct((bsz, s, FOX_WIDTH), BF16),
        jax.ShapeDtypeStruct((bsz, FL_ROWS, s), F32),
    ]
    out_specs = [tok(MLA_HEADS * LANES), tok(MLA_HEADS * LANES), tok(MLA_HEADS * V_DIM),
                 tok(FOX_WIDTH), tok(FOX_WIDTH), tok(FOX_WIDTH),
                 pl.BlockSpec((1, FL_ROWS, ts), lambda b, i: (b, 0, i))]
    return pl.pallas_call(
        _pre_attn_kernel,
        grid=grid,
        in_specs=[tok(d), per_b, per_b, full(g_pre), full(w1), full(wfl), full(bfg), full(g_q),
                  full(wqa), full(wqb), full(g_kv), full(wuk), full(wuv), tok(LANES), tok(LANES)],
        out_specs=out_specs,
        out_shape=outs,
        scratch_shapes=[pltpu.VMEM((FL_ROWS, LANES), F32)],
        compiler_params=_cparams(("arbitrary", "arbitrary")),
        name="pre_attn",
    )(x, sc, sh, g_pre, w1, wfl, bfg, g_q, wqa, wqb, g_kv, wuk, wuv, cp, sp)


def _attn_kernel(*refs, tq, head_lanes, has_bias):
    if has_bias:
        q_ref, k_ref, v_ref, f_ref, o_ref, acc_ref, m_ref, l_ref = refs
    else:
        q_ref, k_ref, v_ref, o_ref, acc_ref, m_ref, l_ref = refs
        f_ref = None
    qi = pl.program_id(2)
    n_heads = acc_ref.shape[0]
    lane = lax.broadcasted_iota(jnp.int32, (tq, LANES), 1)
    qs = []
    for hd in range(n_heads):
        if head_lanes == LANES:
            qs.append(q_ref[0, :, hd * LANES:(hd + 1) * LANES])
        else:
            q2 = q_ref[0, :, (hd // 2) * LANES:(hd // 2 + 1) * LANES]
            keep = (lane < head_lanes) if hd % 2 == 0 else (lane >= head_lanes)
            qs.append(jnp.where(keep, q2, jnp.zeros_like(q2)))

    def step(off, width, masked):
        scores = []
        for hd in range(n_heads):
            pr = hd // 2
            if head_lanes == LANES:
                kk = k_ref[0, pl.ds(off, width), hd * LANES:(hd + 1) * LANES]
            else:
                kk = k_ref[0, pl.ds(off, width), pr * LANES:(pr + 1) * LANES]
            scores.append(_dot_nt(qs[hd], kk))
        for hd in range(n_heads):
            pr = hd // 2
            v2 = v_ref[0, pl.ds(off, width), pr * LANES:(pr + 1) * LANES]
            s = scores[hd]
            if has_bias:
                s = s - f_ref[0, pr, hd % 2:hd % 2 + 1, pl.ds(off, width)]
            if masked:
                rr = lax.broadcasted_iota(jnp.int32, (tq, width), 0)
                cc = lax.broadcasted_iota(jnp.int32, (tq, width), 1)
                s = jnp.where(cc <= rr, s, NEG_INF)
                m_new = jnp.broadcast_to(jnp.max(s, axis=-1, keepdims=True), (tq, LANES))
                p = jnp.exp2(s - jnp.tile(m_new, (1, width // LANES)))
                l_ref[hd] = jnp.broadcast_to(jnp.sum(p, axis=-1, keepdims=True), (tq, LANES))
                m_ref[hd] = m_new
                acc_ref[hd] = _dot(p.astype(BF16), v2)
            else:
                m = m_ref[hd]
                m_new = jnp.maximum(m, jnp.max(s, axis=-1, keepdims=True))
                alpha = jnp.exp2(m - m_new)
                p = jnp.exp2(s - jnp.tile(m_new, (1, width // LANES)))
                l_ref[hd] = alpha * l_ref[hd] + jnp.sum(p, axis=-1, keepdims=True)
                m_ref[hd] = m_new
                acc_ref[hd] = alpha * acc_ref[hd] + _dot(p.astype(BF16), v2)

    step(pl.multiple_of(qi * tq, tq), tq, True)

    per_wide = ATTN_WIDE // tq

    def wide(j, _):
        step(pl.multiple_of(j * ATTN_WIDE, ATTN_WIDE), ATTN_WIDE, False)
        return 0

    def single(j, _):
        step(pl.multiple_of(((qi // per_wide) * per_wide + j) * tq, tq), tq, False)
        return 0

    lax.fori_loop(0, qi // per_wide, wide, 0)
    lax.fori_loop(0, qi % per_wide, single, 0)

    for pr in range(n_heads // 2):
        o0 = acc_ref[2 * pr] / l_ref[2 * pr]
        o1 = acc_ref[2 * pr + 1] / l_ref[2 * pr + 1]
        o_ref[0, :, pr * LANES:(pr + 1) * LANES] = jnp.where(lane < V_DIM, o0, o1).astype(o_ref.dtype)


ATTN_PAIRS = 2
ATTN_WIDE = 1024


def _attention(q, k, v, fcum, tq, head_lanes):
    bsz, s, _ = q.shape
    n_groups = v.shape[2] // (ATTN_PAIRS * LANES)
    qk_w = ATTN_PAIRS * 2 * head_lanes
    v_w = ATTN_PAIRS * LANES
    n_heads = 2 * ATTN_PAIRS
    has_bias = fcum is not None
    in_specs = [
        pl.BlockSpec((1, tq, qk_w), lambda b, p, i: (b, i, p)),
        pl.BlockSpec((1, s, qk_w), lambda b, p, i: (b, 0, p)),
        pl.BlockSpec((1, s, v_w), lambda b, p, i: (b, 0, p)),
    ]
    args = [q, k, v]
    if has_bias:
        in_specs.append(pl.BlockSpec((1, ATTN_PAIRS, 2, s), lambda b, p, i: (b, p, 0, 0)))
        args.append(fcum)
    return pl.pallas_call(
        functools.partial(_attn_kernel, tq=tq, head_lanes=head_lanes, has_bias=has_bias),
        grid=(bsz, n_groups, s // tq),
        in_specs=in_specs,
        out_specs=pl.BlockSpec((1, tq, v_w), lambda b, p, i: (b, i, p)),
        out_shape=jax.ShapeDtypeStruct((bsz, s, v.shape[2]), BF16),
        scratch_shapes=[pltpu.VMEM((n_heads, tq, LANES), F32), pltpu.VMEM((n_heads, tq, LANES), F32),
                        pltpu.VMEM((n_heads, tq, LANES), F32)],
        compiler_params=_cparams(("arbitrary", "arbitrary", "arbitrary")),
        name="attn_fox" if has_bias else "attn_mla",
    )(*args)


def _pack_bf16_pairs(xb):
    w = xb.shape[1] // 2
    lo = lax.bitcast_convert_type(xb[:, :w].astype(F32), jnp.uint32)
    hi = lax.bitcast_convert_type(xb[:, w:].astype(F32), jnp.uint32)
    return (hi & jnp.uint32(0xFFFF0000)) | lax.shift_right_logical(lo, jnp.uint32(16))


def _unpack_bf16_pairs(xw):
    lo = lax.bitcast_convert_type(lax.shift_left(xw, jnp.uint32(16)), F32)
    hi = lax.bitcast_convert_type(xw & jnp.uint32(0xFFFF0000), F32)
    return jnp.concatenate([lo, hi], axis=1).astype(BF16)


def _post_attn_kernel(om_ref, of_ref, x_ref, wo_ref, gpost_ref, gate_ref, gmoe_ref, sc_ref, sh_ref,
                      wr_ref, br_ref, x1_ref, h2_ref, idx_ref, gts_ref):
    half = om_ref.shape[2]
    o = _dot(om_ref[0], wo_ref[0:half, :]) + _dot(of_ref[0], wo_ref[half:, :])
    x1 = x_ref[0] + gate_ref[0] * _rms(o, gpost_ref[...])
    x1_ref[0] = x1
    h2 = _rms(x1, gmoe_ref[...]) * (1.0 + sc_ref[0]) + sh_ref[0]
    hb = h2.astype(BF16)
    h2_ref[0] = _pack_bf16_pairs(hb)
    logits = _dot_nt(wr_ref[...], hb) + br_ref[...]
    n_e = logits.shape[0]
    eid = lax.broadcasted_iota(jnp.int32, logits.shape, 0)
    vals, idxs = [], []
    for _ in range(TOP_K):
        m = jnp.max(logits, axis=0, keepdims=True)
        ix = jnp.min(jnp.where(logits == m, eid, n_e), axis=0, keepdims=True)
        vals.append(m)
        idxs.append(ix)
        logits = jnp.where(eid == ix, -jnp.inf, logits)
    es = [jnp.exp(vv - vals[0]) for vv in vals]
    den = es[0] + es[1] + es[2] + es[3]
    for kk in range(TOP_K):
        idx_ref[0, kk:kk + 1, :] = idxs[kk]
        gts_ref[0, kk:kk + 1, :] = es[kk] / den


def _post_attn(o_mla, o_fox, x, wo, g_post, gate, g_moe, sc, sh, wr_t, br, ts):
    bsz, s, d = x.shape
    half = o_mla.shape[2]
    tok = lambda width: pl.BlockSpec((1, ts, width), lambda b, i: (b, i, 0))
    per_b = pl.BlockSpec((1, 1, d), lambda b, i: (b, 0, 0))
    full = lambda a: pl.BlockSpec(a.shape, lambda b, i: (0,) * a.ndim)
    k_spec = pl.BlockSpec((1, TOP_K, ts), lambda b, i: (b, 0, i))
    return pl.pallas_call(
        _post_attn_kernel,
        grid=(bsz, s // ts),
        in_specs=[tok(half), tok(half), tok(d), full(wo), full(g_post), per_b, full(g_moe), per_b,
                  per_b, full(wr_t), full(br)],
        out_specs=[tok(d), tok(d // 2), k_spec, k_spec],
        out_shape=[jax.ShapeDtypeStruct((bsz, s, d), F32),
                   jax.ShapeDtypeStruct((bsz, s, d // 2), jnp.uint32),
                   jax.ShapeDtypeStruct((bsz, TOP_K, s), jnp.int32),
                   jax.ShapeDtypeStruct((bsz, TOP_K, s), F32)],
        compiler_params=_cparams(("arbitrary", "arbitrary")),
        name="post_attn",
    )(o_mla, o_fox, x, wo, g_post, gate, g_moe, sc, sh, wr_t, br)


def _route_kernel(idx_ref, tri_ref, rank_ref, cnt_ref, run_ref):
    first = jnp.logical_and(pl.program_id(0) == 0, pl.program_id(1) == 0)

    @pl.when(first)
    def _():
        run_ref[...] = jnp.zeros_like(run_ref)

    tr = idx_ref.shape[2]
    eid = lax.broadcasted_iota(jnp.int32, (N_EXPERTS, tr), 0)
    hot = [eid == idx_ref[0, kk:kk + 1, :] for kk in range(TOP_K)]
    multi = hot[0] | hot[1] | hot[2] | hot[3]
    mf = jnp.where(multi, 1.0, 0.0)
    incl = _dot(mf.astype(BF16), tri_ref[...]) + run_ref[:, 0:1]
    excl = incl - mf
    for kk in range(TOP_K):
        rank_ref[0, kk:kk + 1, :] = jnp.sum(jnp.where(hot[kk], excl, 0.0), axis=0,
                                            keepdims=True).astype(jnp.int32)
    run_ref[...] = jnp.broadcast_to(incl[:, tr - 1:tr], run_ref.shape)
    cnt_ref[...] = run_ref[...].astype(jnp.int32)


def _route(idx_kt, tr):
    bsz, _, s = idx_kt.shape
    tri = (jnp.arange(tr)[:, None] <= jnp.arange(tr)[None, :]).astype(BF16)
    k_spec = pl.BlockSpec((1, TOP_K, tr), lambda b, i: (b, 0, i))
    return pl.pallas_call(
        _route_kernel,
        grid=(bsz, s // tr),
        in_specs=[k_spec, pl.BlockSpec((tr, tr), lambda b, i: (0, 0))],
        out_specs=[k_spec, pl.BlockSpec((N_EXPERTS, LANES), lambda b, i: (0, 0))],
        out_shape=[jax.ShapeDtypeStruct((bsz, TOP_K, s), jnp.int32),
                   jax.ShapeDtypeStruct((N_EXPERTS, LANES), jnp.int32)],
        scratch_shapes=[pltpu.VMEM((N_EXPERTS, LANES), F32)],
        compiler_params=_cparams(("arbitrary", "arbitrary")),
        name="route",
    )(idx_kt, tri)


SC_CORES = 2
SC_SUBCORES = 16
SC_CHUNK = 64
MOE_GROUP_FRACTIONS = (1 / 18, 1 / 6, 1 / 3)


def _sc_gather(table, idx):
    m = idx.shape[0]
    width = table.shape[1]
    n_workers = SC_CORES * SC_SUBCORES
    per_worker = m // n_workers
    n_chunks = per_worker // SC_CHUNK
    assert n_chunks * SC_CHUNK * n_workers == m
    n_buf = next(nb for nb in (3, 2, 1) if n_chunks % nb == 0)
    mesh = plsc.VectorSubcoreMesh(core_axis_name="c", subcore_axis_name="s")

    @functools.partial(
        pl.kernel, mesh=mesh,
        out_type=jax.ShapeDtypeStruct((m, width), table.dtype),
        scratch_types=[pltpu.VMEM((n_chunks, SC_CHUNK), jnp.int32),
                       pltpu.VMEM((n_buf, SC_CHUNK, width), table.dtype),
                       pltpu.SemaphoreType.DMA((n_buf,)),
                       pltpu.SemaphoreType.DMA((n_buf,))],
        name="sc_gather",
    )
    def gather(table_hbm, idx_hbm, out_hbm, idx_v, rows_v, sem_g, sem_w):
        wid = lax.axis_index("s") * SC_CORES + lax.axis_index("c")
        base = wid * per_worker
        pltpu.sync_copy(idx_hbm.at[wid], idx_v)

        def gather_copy(j, b):
            return pltpu.make_async_copy(table_hbm.at[idx_v.at[j]], rows_v.at[b], sem_g.at[b])

        for b in range(n_buf):
            gather_copy(b, b).start()

        @pl.loop(0, n_chunks, step=n_buf)
        def _(g):
            for b in range(n_buf):
                j = g + b
                gather_copy(j, b).wait()
                off = pl.multiple_of(base + j * SC_CHUNK, SC_CHUNK)
                write = pltpu.make_async_copy(rows_v.at[b], out_hbm.at[pl.ds(off, SC_CHUNK)],
                                              sem_w.at[b])
                write.start()
                write.wait()

                @pl.when(j + n_buf < n_chunks)
                def _():
                    gather_copy(j + n_buf, b).start()

    return gather(table, idx.reshape(n_workers, n_chunks, SC_CHUNK))


def _expert_mlp(xb, wgu_bf, bgu_ref, wd_bf, bd_ref):
    ff = wd_bf.shape[0]
    gu = _dot(xb, wgu_bf[...]) + bgu_ref[0]
    g = jnp.minimum(gu[:, :ff], SWIGLU_LIMIT)
    u = jnp.clip(gu[:, ff:], -SWIGLU_LIMIT, SWIGLU_LIMIT)
    glu = g * jax.nn.sigmoid(SWIGLU_ALPHA * g)
    act = ((u + 1.0) * glu).astype(BF16)
    return _dot(act, wd_bf[...]) + bd_ref[0]


def _expert_pregathered_kernel(be_ref, first_ref, n_ref, sidx_ref, x_ref, wgu_ref, bgu_ref, wd_ref,
                               bd_ref, *rest):
    y_hbm, ybuf0, ybuf1, wgu_bf, wd_bf, sem_s = rest[-6:]
    i = pl.program_id(0)
    n = n_ref[0]
    bm = ybuf0.shape[0]
    ybufs = (ybuf0, ybuf1)

    def scatter_start(slot, r, priority=0):
        pltpu.make_async_copy(ybufs[slot].at[pl.ds(r, 1), :],
                              y_hbm.at[pl.ds(sidx_ref[0, 0, r], 1), :],
                              sem_s.at[slot]).start(priority)

    def scatter_wait(slot):
        pltpu.make_async_copy(ybufs[slot], y_hbm.at[pl.ds(0, bm), :], sem_s.at[slot]).wait()

    @pl.when(jnp.logical_and(first_ref[i] == 1, i < n))
    def _():
        wgu_bf[...] = wgu_ref[0].astype(BF16)
        wd_bf[...] = wd_ref[0].astype(BF16)

    def step(par):
        if par == 0:
            @pl.when(i == 0)
            def _():
                ybuf1[...] = jnp.zeros_like(ybuf1)

        @pl.when(jnp.logical_and(i >= 1, i <= n))
        def _():
            scatter_wait(par)

        @pl.when(i < n)
        def _():
            xb = _unpack_bf16_pairs(x_ref[...])
            for r in range(bm):
                scatter_start(1 - par, r, 1)
            ybufs[par][...] = _expert_mlp(xb, wgu_bf, bgu_ref, wd_bf, bd_ref)

        @pl.when(i == n)
        def _():
            def body(r, _):
                scatter_start(1 - par, r)
                return 0
            lax.fori_loop(0, bm, body, 0, unroll=8)
            scatter_wait(1 - par)

    @pl.when(i % 2 == 0)
    def _():
        step(0)

    @pl.when(i % 2 == 1)
    def _():
        step(1)


def _experts_pregathered(be_step, first_step, n_real, sidx, xs, w_gu, b_gu, w_d, b_d, y4, n_out_rows,
                         bm):
    e, d, f2 = w_gu.shape
    ff = w_d.shape[1]
    n_steps = sidx.shape[0]
    n_xblocks = xs.shape[0] // bm
    n_prefetch = 3
    prev = [] if y4 is None else [y4]
    return pl.pallas_call(
        _expert_pregathered_kernel,
        grid_spec=pltpu.PrefetchScalarGridSpec(
            num_scalar_prefetch=n_prefetch,
            grid=(n_steps,),
            in_specs=[
                pl.BlockSpec((1, 1, bm), lambda i, be, fi, nu: (i, 0, 0), memory_space=pltpu.SMEM),
                pl.BlockSpec((bm, d // 2), lambda i, be, fi, nu: (jnp.minimum(i, n_xblocks - 1), 0)),
                pl.BlockSpec((1, d, f2), lambda i, be, fi, nu: (be[i], 0, 0)),
                pl.BlockSpec((1, 1, f2), lambda i, be, fi, nu: (be[i], 0, 0)),
                pl.BlockSpec((1, ff, d), lambda i, be, fi, nu: (be[i], 0, 0)),
                pl.BlockSpec((1, 1, d), lambda i, be, fi, nu: (be[i], 0, 0)),
            ] + [pl.BlockSpec(memory_space=pl.ANY) for _ in prev],
            out_specs=pl.BlockSpec(memory_space=pl.ANY),
            scratch_shapes=[pltpu.VMEM((bm, d), F32), pltpu.VMEM((bm, d), F32),
                            pltpu.VMEM((d, f2), BF16), pltpu.VMEM((ff, d), BF16),
                            pltpu.SemaphoreType.DMA((2,))],
        ),
        out_shape=jax.ShapeDtypeStruct((n_out_rows, d), F32),
        input_output_aliases={n_prefetch + 6: 0} if prev else {},
        compiler_params=_cparams(("arbitrary",)),
        name="experts",
    )(be_step, first_step, n_real, sidx, xs, w_gu, b_gu.reshape(e, 1, f2), w_d, b_d.reshape(e, 1, d),
      *prev)


def _combine_kernel(y0_ref, y1_ref, y2_ref, y3_ref, gts_ref, x1_ref, gpost_ref, gate_ref, o_ref):
    gts = gts_ref[...]
    acc = gts[:, 0:1] * y0_ref[...]
    for kk, y_ref in enumerate((y1_ref, y2_ref, y3_ref), start=1):
        acc = acc + gts[:, kk:kk + 1] * y_ref[...]
    o_ref[0] = x1_ref[0] + gate_ref[0] * _rms(acc, gpost_ref[...])


def _combine(y4, gts, x1, g_post, gate, tt):
    bsz, s, d = x1.shape
    n_t = s // tt
    n_tiles = bsz * n_t
    y_spec = lambda kk: pl.BlockSpec((tt, d), lambda b, i: (kk * n_tiles + b * n_t + i, 0))
    return pl.pallas_call(
        _combine_kernel,
        grid=(bsz, n_t),
        in_specs=[
            y_spec(0), y_spec(1), y_spec(2), y_spec(3),
            pl.BlockSpec((tt, TOP_K), lambda b, i: (b * n_t + i, 0)),
            pl.BlockSpec((1, tt, d), lambda b, i: (b, i, 0)),
            pl.BlockSpec(g_post.shape, lambda b, i: (0, 0)),
            pl.BlockSpec((1, 1, d), lambda b, i: (b, 0, 0)),
        ],
        out_specs=pl.BlockSpec((1, tt, d), lambda b, i: (b, i, 0)),
        out_shape=jax.ShapeDtypeStruct((bsz, s, d), F32),
        compiler_params=_cparams(("arbitrary", "arbitrary")),
        name="combine",
    )(y4, y4, y4, y4, gts, x1, g_post, gate)


def _rope_patterns(positions):
    inv_freq = ROPE_THETA ** (-jnp.arange(0, ROPE_DIM, 2, dtype=F32) / ROPE_DIM)
    hr = ROPE_DIM // 2
    lane = jnp.arange(LANES)
    lo_half = (lane >= NOPE_DIM) & (lane < NOPE_DIM + hr)
    hi_half = (lane >= NOPE_DIM + hr) & (lane < NOPE_DIM + ROPE_DIM)
    freq = jnp.zeros((LANES,), F32).at[NOPE_DIM:NOPE_DIM + hr].set(inv_freq)
    freq = freq.at[NOPE_DIM + hr:NOPE_DIM + ROPE_DIM].set(inv_freq)
    ang = positions.astype(F32)[..., None] * freq
    cp = jnp.where(lane < NOPE_DIM, 1.0, jnp.where(lo_half | hi_half, jnp.cos(ang), 0.0))
    sp = jnp.sin(ang) * jnp.where(lo_half, -1.0, jnp.where(hi_half, 1.0, 0.0))
    return cp.astype(F32), sp.astype(F32)


def _prep_mixer_weights(w_in, w_uq, w_ukv, b_forget):
    d = w_in.shape[0]
    o_kr = Q_RANK + KV_RANK
    o_f = o_kr + ROPE_DIM
    hr = ROPE_DIM // 2
    z = lambda n: jnp.zeros((d, n), w_in.dtype)
    kr = w_in[:, o_kr:o_f]
    kr_sw = jnp.concatenate([kr[:, hr:], kr[:, :hr]], axis=1)
    pad = LANES - NOPE_DIM - ROPE_DIM
    w1 = jnp.concatenate([
        w_in[:, :o_kr],
        z(NOPE_DIM), kr, z(pad),
        z(NOPE_DIM), kr_sw, z(pad),
        w_in[:, o_f:o_f + 3 * FOX_WIDTH],
    ], axis=1).astype(BF16)
    wfl = jnp.zeros((FL_ROWS, d), w_in.dtype).at[:FOX_HEADS].set(
        w_in[:, o_f + 3 * FOX_WIDTH:].T).astype(BF16)
    bfg = jnp.zeros((FL_ROWS, 1), F32).at[:FOX_HEADS, 0].set(b_forget)

    wq = w_uq.reshape(Q_RANK, MLA_HEADS, NOPE_DIM + ROPE_DIM)
    nope, rope = wq[..., :NOPE_DIM], wq[..., NOPE_DIM:]
    zq = lambda n: jnp.zeros((Q_RANK, MLA_HEADS, n), w_uq.dtype)
    wqa = jnp.concatenate([nope, rope, zq(pad)], axis=-1).reshape(Q_RANK, -1).astype(BF16)
    wqb = jnp.concatenate([zq(NOPE_DIM), rope[..., hr:], rope[..., :hr], zq(pad)],
                          axis=-1).reshape(Q_RANK, -1).astype(BF16)
    wkv = w_ukv.reshape(KV_RANK, MLA_HEADS, NOPE_DIM + V_DIM)
    wuk = jnp.concatenate([wkv[..., :NOPE_DIM],
                           jnp.zeros((KV_RANK, MLA_HEADS, LANES - NOPE_DIM), w_ukv.dtype)],
                          axis=-1).reshape(KV_RANK, -1).astype(BF16)
    wuv = wkv[..., NOPE_DIM:].reshape(KV_RANK, -1).astype(BF16)
    return w1, wfl, bfg, wqa, wqb, wuk, wuv


def _slot_tables(idx_kt, rank_kt, counts, bm):
    bsz, _, s = idx_kt.shape
    t = bsz * s
    n_blocks = (t * TOP_K) // bm + N_EXPERTS
    n_real = t * TOP_K
    i32 = jnp.int32
    eids = jnp.arange(N_EXPERTS, dtype=i32)

    pcounts = ((counts + bm - 1) // bm) * bm
    incl_mat = (eids[:, None] <= eids[None, :]).astype(i32)
    pends = pcounts @ incl_mat
    pstarts = pends - pcounts
    cstarts = counts @ incl_mat - counts
    total = pends[-1]
    n_used = (total // bm).astype(i32).reshape(1)

    onehot = idx_kt[..., None] == eids
    dest = jnp.sum(jnp.where(onehot, pstarts, 0), axis=-1) + rank_kt
    tok = jnp.arange(t, dtype=i32).reshape(bsz, 1, s)
    kk = jnp.arange(TOP_K, dtype=i32).reshape(1, TOP_K, 1)
    yrow_real = jnp.broadcast_to(kk * t + tok, dest.shape)
    _, rows_sorted = lax.sort((dest.reshape(-1), yrow_real.reshape(-1)), num_keys=1)

    blk0 = jnp.arange(n_blocks, dtype=i32)
    be_blk = jnp.minimum(jnp.sum((pends[None, :] <= (blk0 * bm)[:, None]).astype(i32), axis=1),
                         N_EXPERTS - 1)
    hot_b = be_blk[:, None] == eids[None, :]
    pick = lambda v: jnp.sum(jnp.where(hot_b, v[None, :], 0), axis=1)[:, None]
    local = (blk0 * bm)[:, None] + jnp.arange(bm, dtype=i32)[None, :] - pick(pstarts)
    valid = jnp.logical_and(local < pick(counts), (blk0 < n_used[0])[:, None])
    j = jnp.clip(pick(cstarts) + local, 0, n_real - 1)
    rows = jnp.take(rows_sorted, j.reshape(-1), axis=0).reshape(n_blocks, bm)
    spare = TOP_K * t + jnp.arange(bm, dtype=i32)[None, :]
    yrow = jnp.where(valid, rows, spare)
    slot = (blk0 * bm)[:, None] + jnp.arange(bm, dtype=i32)[None, :]
    src_tok = jnp.where(valid, rows % t, slot % t)

    return src_tok, yrow, be_blk, n_used[0]


def _first_flags(be_step):
    return jnp.concatenate([jnp.ones((1,), jnp.int32),
                            (be_step[1:] != be_step[:-1]).astype(jnp.int32)])


def _group_sizes(n_blocks, bm):
    unit = 3 * (SC_CORES * SC_SUBCORES * SC_CHUNK) // bm
    sizes = [max(unit, (int(n_blocks * f) // unit) * unit) for f in MOE_GROUP_FRACTIONS]
    sizes.append(n_blocks - sum(sizes))
    assert sizes[-1] > 0 and sizes[-1] % unit == 0
    return sizes


def _moe_blocks(src_tok, yrow, be_blk, n_used, h2p, w_gu, b_gu, w_d, b_d, n_out_rows, bm):
    i32 = jnp.int32
    n_blocks = src_tok.shape[0]
    spare = (n_out_rows - bm) + jnp.arange(bm, dtype=i32)[None, :]
    y4 = None
    lo = 0
    for nb in _group_sizes(n_blocks, bm):
        xs = _sc_gather(h2p, src_tok[lo:lo + nb].reshape(-1))
        sidx = jnp.concatenate([spare, yrow[lo:lo + nb]]).reshape(nb + 1, 1, bm)
        be = be_blk[lo + jnp.minimum(jnp.arange(nb + 1, dtype=i32), nb - 1)]
        n_g = jnp.clip(n_used - lo, 0, nb).astype(i32).reshape(1)
        y4 = _experts_pregathered(be, _first_flags(be), n_g, sidx, xs, w_gu, b_gu, w_d, b_d, y4,
                                  n_out_rows, bm)
        lo += nb
    return y4


def _layer(x, ada, cp, sp, g_attn_pre, g_attn_post, w_in, g_q_norm, w_uq, g_kv_norm, w_ukv,
           b_forget, w_out, g_moe_pre, g_moe_post, w_router, b_router, w_gate_up, b_gate_up,
           w_down, b_down):
    bsz, s, d = x.shape
    t = bsz * s
    ts = min(1024, s)
    tq = min(512, s)
    tt = min(512, s)
    tr = min(1024, s)
    bm = 512
    row = lambda v: v.reshape(1, -1)
    sh_a, sc_a, gt_a, sh_m, sc_m, gt_m = [v.reshape(bsz, 1, d) for v in jnp.split(ada, 6, axis=-1)]

    w1, wfl, bfg, wqa, wqb, wuk, wuv = _prep_mixer_weights(w_in, w_uq, w_ukv, b_forget)
    q, k, v, fq, fk, fv, fcum = _pre_attn(x, sc_a, sh_a, row(g_attn_pre), w1, wfl, bfg,
                                          row(g_q_norm), wqa, wqb, row(g_kv_norm), wuk, wuv,
                                          cp, sp, ts)
    o_mla = _attention(q, k, v, None, tq, LANES)
    fpairs = fcum[:, :FOX_HEADS].reshape(bsz, FOX_HEADS // 2, 2, s)
    o_fox = _attention(fq, fk, fv, fpairs, tq, FOX_DIM)

    x1, h2, idx_kt, gts_kt = _post_attn(o_mla, o_fox, x, w_out.astype(BF16), row(g_attn_post), gt_a,
                                        row(g_moe_pre), sc_m, sh_m, w_router.T.astype(BF16),
                                        b_router.reshape(-1, 1), ts)

    rank_kt, cnt = _route(idx_kt, tr)
    src_tok, yrow, be_blk, n_used = _slot_tables(idx_kt, rank_kt, cnt[:, 0], bm)
    y4 = _moe_blocks(src_tok, yrow, be_blk, n_used, h2.reshape(t, d // 2), w_gate_up, b_gate_up,
                     w_down, b_down, TOP_K * t + bm, bm)
    gts = gts_kt.transpose(0, 2, 1).reshape(t, TOP_K)
    return _combine(y4, gts, x1, row(g_moe_post), gt_m, tt)


def kernel(x, c, positions, w_ada, b_ada, g_attn_pre, g_attn_post, w_in, g_q_norm, w_uq, g_kv_norm,
           w_ukv, b_forget, w_out, g_moe_pre, g_moe_post, w_router, b_router, w_gate_up, b_gate_up,
           w_down, b_down):
    cp, sp = _rope_patterns(positions)
    for layer in range(w_ada.shape[0]):
        ada = _ada(c, w_ada[layer], b_ada[layer])
        x = _layer(x, ada, cp, sp, g_attn_pre[layer], g_attn_post[layer], w_in[layer],
                   g_q_norm[layer], w_uq[layer], g_kv_norm[layer], w_ukv[layer], b_forget[layer],
                   w_out[layer], g_moe_pre[layer], g_moe_post[layer], w_router[layer],
                   b_router[layer], w_gate_up[layer], b_gate_up[layer], w_down[layer],
                   b_down[layer])
    return x
```
